```python
import math
import jax
import jax.numpy as jnp
from jax import lax
import numpy as np

D_MODEL = 1024
BATCH = 2
SEQ = 16384
DEPTH = 2
DEC_BATCH = 1
DEC_SEQ = 16384
PAST_LEN = 128

GDN_HEADS = 4
GDN_DK = 128
GDN_DV = 128
SSD_HEADS = 8
SSD_HEADDIM = 64
SSD_GROUPS = 2
SSD_STATE = 128
MLSTM_HEADS = 4
MLSTM_DH = 128

GDN_WIDTH = GDN_HEADS * GDN_DV
GDN_QK = GDN_HEADS * GDN_DK
SSD_WIDTH = SSD_HEADS * SSD_HEADDIM
SSD_BC = SSD_GROUPS * SSD_STATE
MLSTM_WIDTH = MLSTM_HEADS * MLSTM_DH
D_MIX = GDN_WIDTH + SSD_WIDTH + MLSTM_WIDTH
N_DIR = 2
CONV_W = 4
CONV_PAD_L = (CONV_W - 1) // 2
CONV_PAD_R = CONV_W - 1 - CONV_PAD_L
CHUNK = 64
D_FF = 2816
N_EXPERTS = 8
TOP_K = 2
D_EXPERT = 3584
N_DENSE = (DEPTH + 1) // 2
N_MOE = DEPTH // 2
DN_ALPHA = (2 * DEPTH) ** 0.25
DN_BETA = (8 * DEPTH) ** -0.25
LN_EPS = 1e-5
NORM_EPS = 1e-6

IN_SPLITS = (
    GDN_QK, GDN_QK, GDN_WIDTH, GDN_WIDTH, N_DIR * GDN_HEADS, N_DIR * GDN_HEADS,
    SSD_WIDTH, SSD_WIDTH, SSD_BC, SSD_BC, N_DIR * SSD_HEADS,
    MLSTM_WIDTH, MLSTM_WIDTH, MLSTM_WIDTH, MLSTM_WIDTH,
    N_DIR * MLSTM_HEADS, N_DIR * MLSTM_HEADS,
)
D_IN = sum(IN_SPLITS)

kernel_name = 'hymba_gdn_ssd_mlstm_deepnorm_encoder'


def _split_cols(p, sizes):
    idx = np.cumsum(sizes)[:-1].tolist()
    return jnp.split(p, idx, axis=-1)


def _layernorm(x, g, b):
    xf = x.astype(jnp.float32)
    mu = jnp.mean(xf, -1, keepdims=True)
    xc = xf - mu
    var = jnp.mean(xc * xc, -1, keepdims=True)
    return (xc * lax.rsqrt(var + LN_EPS) * g + b).astype(x.dtype)


def _rmsnorm(x, w):
    return x * lax.rsqrt(jnp.mean(x * x, -1, keepdims=True) + NORM_EPS) * w


def _l2norm(x):
    return x * lax.rsqrt(jnp.sum(x * x, -1, keepdims=True) + NORM_EPS)


def _dwconv_centred(x, w):
    return lax.conv_general_dilated(
        x, w[:, None, :], window_strides=(1,), padding=((CONV_PAD_L, CONV_PAD_R),),
        dimension_numbers=('NWC', 'WIO', 'NWC'), feature_group_count=x.shape[-1])


def _to_chunks(t):
    b, L = t.shape[:2]
    return jnp.swapaxes(t.reshape(b, L // CHUNK, CHUNK, *t.shape[2:]), 2, 3)


def _from_chunks(t):
    b, n, h, c = t.shape[:4]
    return jnp.swapaxes(t, 2, 3).reshape(b, n * c, h, *t.shape[4:])


def _bidirectional(scan_fn, shared, per_dir):
    rev = lambda t: jnp.flip(t, 1)
    fwd = scan_fn(*shared, *(t[:, :, 0] for t in per_dir))
    bwd = scan_fn(*(rev(t) for t in shared), *(rev(t[:, :, 1]) for t in per_dir))
    return fwd + rev(bwd)


def _gated_delta_dir(q, k, v, beta, g):
    q, k, v, beta, g = map(_to_chunks, (q, k, v, beta, g))
    incl = jnp.tril(jnp.ones((CHUNK, CHUNK), dtype=bool))
    strict = jnp.tril(jnp.ones((CHUNK, CHUNK), dtype=bool), -1)
    G = jnp.cumsum(g, -1)
    decay = jnp.where(incl, jnp.exp(jnp.where(incl, G[..., :, None] - G[..., None, :], 0.0)), 0.0)
    kb = k * beta[..., None]
    A = jnp.where(strict, jnp.einsum('bchtd,bchsd->bchts', kb, k) * decay, 0.0)
    T = A + jnp.eye(CHUNK, dtype=A.dtype)
    u = lax.linalg.triangular_solve(T, v * beta[..., None], left_side=True, lower=True, unit_diagonal=True)
    w = lax.linalg.triangular_solve(T, kb * jnp.exp(G)[..., None], left_side=True, lower=True, unit_diagonal=True)
    qk = jnp.einsum('bchtd,bchsd->bchts', q, k) * decay
    q_dec = q * jnp.exp(G)[..., None]
    k_dec = k * jnp.exp(G[..., -1:] - G)[..., None]
    g_end = jnp.exp(G[..., -1])

    def step(S, inp):
        u_c, w_c, qk_c, qd_c, kd_c, ge_c = inp
        v_new = u_c - jnp.einsum('bhld,bhde->bhle', w_c, S)
        o = jnp.einsum('bhld,bhde->bhle', qd_c, S) + jnp.einsum('bhts,bhse->bhte', qk_c, v_new)
        S = S * ge_c[..., None, None] + jnp.einsum('bhld,bhle->bhde', kd_c, v_new)
        return S, o

    b, _, h, _, dk = q.shape
    S0 = jnp.zeros((b, h, dk, v.shape[-1]), u.dtype)
    xs = tuple(jnp.moveaxis(t, 1, 0) for t in (u, w, qk, q_dec, k_dec, g_end))
    _, o = lax.scan(step, S0, xs)
    return _from_chunks(jnp.moveaxis(o, 0, 1))


def _ssd_dir(x, Bm, Cm, dt, adt):
    b, L, H, P = x.shape
    n = L // CHUNK
    E = H // SSD_GROUPS
    xdt = (x * dt[..., None]).reshape(b, n, CHUNK, SSD_GROUPS, E, P)
    Bc = Bm.reshape(b, n, CHUNK, SSD_GROUPS, SSD_STATE)
    Cc = Cm.reshape(b, n, CHUNK, SSD_GROUPS, SSD_STATE)
    acs = jnp.cumsum(jnp.moveaxis(adt.reshape(b, n, CHUNK, SSD_GROUPS, E), 2, -1), -1)
    incl = jnp.tril(jnp.ones((CHUNK, CHUNK), dtype=bool))
    Lmat = jnp.where(incl, jnp.exp(jnp.where(incl, acs[..., :, None] - acs[..., None, :], 0.0)), 0.0)
    CB = jnp.einsum('bctgn,bcsgn->bcgts', Cc, Bc)
    y_diag = jnp.einsum('bcgts,bcgets,bcsgep->bctgep', CB, Lmat, xdt)
    decay_to_end = jnp.exp(acs[..., -1:] - acs)
    chunk_states = jnp.einsum('bclgn,bcgel,bclgep->bcgepn', Bc, decay_to_end, xdt)
    chunk_decay = jnp.exp(acs[..., -1])

    def step(hs, inp):
        st, cd = inp
        return hs * cd[..., None, None] + st, hs

    h0 = jnp.zeros((b, SSD_GROUPS, E, P, SSD_STATE), xdt.dtype)
    _, h_start = lax.scan(step, h0, (jnp.moveaxis(chunk_states, 1, 0), jnp.moveaxis(chunk_decay, 1, 0)))
    h_start = jnp.moveaxis(h_start, 0, 1)
    y_off = jnp.einsum('bclgn,bcgepn,bcgel->bclgep', Cc, h_start, jnp.exp(acs))
    return (y_diag + y_off).reshape(b, L, H, P)


def _mlstm_dir(q, k, v, i_pre, log_f):
    q, k, v, i_pre, log_f = map(_to_chunks, (q, k, v, i_pre, log_f))
    incl = jnp.tril(jnp.ones((CHUNK, CHUNK), dtype=bool))
    bcum = jnp.cumsum(log_f, -1)
    logD = jnp.where(incl, bcum[..., :, None] - bcum[..., None, :] + i_pre[..., None, :], -jnp.inf)
    m_intra = jnp.max(logD, -1)
    qkD = jnp.einsum('bchtd,bchsd->bchts', q, k) * jnp.exp(logD - m_intra[..., None])
    num_intra = jnp.einsum('bchts,bchse->bchte', qkD, v)
    den_intra = jnp.sum(qkD, -1)
    log_w = bcum[..., -1:] - bcum + i_pre
    m_chunk = jnp.max(log_w, -1)
    w_s = jnp.exp(log_w - m_chunk[..., None])
    kv = jnp.einsum('bchl,bchld,bchle->bchde', w_s, k, v)
    ksum = jnp.einsum('bchl,bchld->bchd', w_s, k)
    g_tot = bcum[..., -1]

    def step(carry, inp):
        Cs, ns, ms = carry
        kv_c, ks_c, mc_c, g_c = inp
        m_new = jnp.maximum(g_c + ms, mc_c)
        s_old = jnp.exp(g_c + ms - m_new)
        s_add = jnp.exp(mc_c - m_new)
        C_new = Cs * s_old[..., None, None] + kv_c * s_add[..., None, None]
        n_new = ns * s_old[..., None] + ks_c * s_add[..., None]
        return (C_new, n_new, m_new), (Cs, ns, ms)

    b, _, h, _, d = q.shape
    init = (jnp.zeros((b, h, d, v.shape[-1]), kv.dtype), jnp.zeros((b, h, d), kv.dtype), jnp.zeros((b, h), kv.dtype))
    xs = tuple(jnp.moveaxis(t, 1, 0) for t in (kv, ksum, m_chunk, g_tot))
    _, (C0, n0, m0) = lax.scan(step, init, xs)
    C0, n0, m0 = (jnp.moveaxis(t, 0, 1) for t in (C0, n0, m0))
    m_inter = bcum + m0[..., None]
    m_t = jnp.maximum(m_inter, m_intra)
    s_inter = jnp.exp(m_inter - m_t)
    s_intra = jnp.exp(m_intra - m_t)
    num = s_inter[..., None] * jnp.einsum('bchld,bchde->bchle', q, C0) + s_intra[..., None] * num_intra
    den = s_inter * jnp.einsum('bchld,bchd->bchl', q, n0) + s_intra * den_intra
    hid = num / jnp.maximum(jnp.abs(den), jnp.exp(-m_t))[..., None]
    return _from_chunks(hid)


def _mixer(x, w_in, conv_a_w, a_A_log, a_dt_bias, a_norm_w, conv_b_w, conv_b_b,
           b_A_log, b_dt_bias, b_D, b_norm_w, c_i_bias, c_f_bias, c_norm_w, w_out):
    f32 = jnp.float32
    bsz, L, _ = x.shape
    proj = jnp.einsum('bld,de->ble', x, w_in).astype(f32)
    (a_q, a_k, a_v, a_z, a_b, a_a, s_z, s_x, s_B, s_C, s_dt,
     c_q, c_k, c_v, c_o, c_i, c_f) = _split_cols(proj, IN_SPLITS)

    qkv = jax.nn.silu(_dwconv_centred(jnp.concatenate([a_q, a_k, a_v], -1), conv_a_w.astype(f32)))
    a_q, a_k, a_v = _split_cols(qkv, (GDN_QK, GDN_QK, GDN_WIDTH))
    a_q = _l2norm(a_q.reshape(bsz, L, GDN_HEADS, GDN_DK)) * (GDN_DK ** -0.5)
    a_k = _l2norm(a_k.reshape(bsz, L, GDN_HEADS, GDN_DK))
    a_v = a_v.reshape(bsz, L, GDN_HEADS, GDN_DV)
    beta = jax.nn.sigmoid(a_b.reshape(bsz, L, N_DIR, GDN_HEADS))
    g = -jnp.exp(a_A_log.astype(f32)) * jax.nn.softplus(a_a.reshape(bsz, L, N_DIR, GDN_HEADS) + a_dt_bias.astype(f32))
    o_a = _bidirectional(_gated_delta_dir, (a_q, a_k, a_v), (beta, g))
    y_a = (_rmsnorm(o_a, a_norm_w.astype(f32)) * jax.nn.silu(a_z.reshape(bsz, L, GDN_HEADS, GDN_DV))).reshape(bsz, L, GDN_WIDTH)

    xbc = jax.nn.silu(_dwconv_centred(jnp.concatenate([s_x, s_B, s_C], -1), conv_b_w.astype(f32)) + conv_b_b.astype(f32))
    s_x, s_B, s_C = _split_cols(xbc, (SSD_WIDTH, SSD_BC, SSD_BC))
    s_x = s_x.reshape(bsz, L, SSD_HEADS, SSD_HEADDIM)
    s_B = s_B.reshape(bsz, L, SSD_GROUPS, SSD_STATE)
    s_C = s_C.reshape(bsz, L, SSD_GROUPS, SSD_STATE)
    dt = jax.nn.softplus(s_dt.reshape(bsz, L, N_DIR, SSD_HEADS) + b_dt_bias.astype(f32))
    adt = -jnp.exp(b_A_log.astype(f32)) * dt
    y_s = _bidirectional(_ssd_dir, (s_x, s_B, s_C), (dt, adt)) + s_x * b_D.astype(f32)[:, None]
    y_s = (y_s.reshape(bsz, L, SSD_WIDTH) * jax.nn.silu(s_z)).reshape(bsz, L, SSD_GROUPS, SSD_WIDTH // SSD_GROUPS)
    y_b = _rmsnorm(y_s, b_norm_w.astype(f32).reshape(SSD_GROUPS, SSD_WIDTH // SSD_GROUPS)).reshape(bsz, L, SSD_WIDTH)

    c_q = c_q.reshape(bsz, L, MLSTM_HEADS, MLSTM_DH)
    c_k = c_k.reshape(bsz, L, MLSTM_HEADS, MLSTM_DH) * (MLSTM_DH ** -0.5)
    c_v = c_v.reshape(bsz, L, MLSTM_HEADS, MLSTM_DH)
    i_pre = c_i.reshape(bsz, L, N_DIR, MLSTM_HEADS) + c_i_bias.astype(f32)
    log_f = jax.nn.log_sigmoid(c_f.reshape(bsz, L, N_DIR, MLSTM_HEADS) + c_f_bias.astype(f32))
    hc = _bidirectional(_mlstm_dir, (c_q, c_k, c_v), (i_pre, log_f))
    mu = jnp.mean(hc, -1, keepdims=True)
    hc = hc - mu
    hc = hc * lax.rsqrt(jnp.mean(hc * hc, -1, keepdims=True) + NORM_EPS) * c_norm_w.astype(f32).reshape(MLSTM_HEADS, MLSTM_DH)
    y_c = (jax.nn.sigmoid(c_o).reshape(bsz, L, MLSTM_HEADS, MLSTM_DH) * hc).reshape(bsz, L, MLSTM_WIDTH)

    y = jnp.concatenate([y_a, y_b, y_c], -1).astype(x.dtype)
    return jnp.einsum('ble,ed->bld', y, w_out)


def _swiglu(x, wg, wu, wd):
    return jnp.einsum('blf,fd->bld', jax.nn.silu(jnp.einsum('bld,df->blf', x, wg)) * jnp.einsum('bld,df->blf', x, wu), wd)


def _moe(x, router, wg, wu, wd):
    logits = jnp.einsum('bld,de->ble', x, router).astype(jnp.float32)
    top_v, top_i = lax.top_k(logits, TOP_K)
    gates = jax.nn.softmax(top_v, axis=-1)
    combine = jnp.sum(jax.nn.one_hot(top_i, N_EXPERTS, dtype=jnp.float32) * gates[..., None], axis=-2)
    out = jnp.zeros(x.shape, jnp.float32)
    for e in range(N_EXPERTS):
        out = out + combine[..., e:e + 1] * _swiglu(x, wg[e], wu[e], wd[e]).astype(jnp.float32)
    return out.astype(x.dtype)


def _trunk(x, w_in, conv_a_w, a_A_log, a_dt_bias, a_norm_w, conv_b_w, conv_b_b, b_A_log, b_dt_bias,
           b_D, b_norm_w, c_i_bias, c_f_bias, c_norm_w, w_out, ln1_g, ln1_b, ln2_g, ln2_b,
           ffn_w_gate, ffn_w_up, ffn_w_down, moe_router, moe_w_gate, moe_w_up, moe_w_down):
    for l in range(DEPTH):
        h = _mixer(x, w_in[l], conv_a_w[l], a_A_log[l], a_dt_bias[l], a_norm_w[l], conv_b_w[l], conv_b_b[l],
                   b_A_log[l], b_dt_bias[l], b_D[l], b_norm_w[l], c_i_bias[l], c_f_bias[l], c_norm_w[l], w_out[l])
        x = _layernorm(DN_ALPHA * x + h, ln1_g[l], ln1_b[l])
        j = l // 2
        if l % 2 == 0:
            f = _swiglu(x, ffn_w_gate[j], ffn_w_up[j], ffn_w_down[j])
        else:
            f = _moe(x, moe_router[j], moe_w_gate[j], moe_w_up[j], moe_w_down[j])
        x = _layernorm(DN_ALPHA * x + f, ln2_g[l], ln2_b[l])
    return x


def setup_inputs(seed: int = 0) -> dict:
    key = jax.random.key(seed)
    keys = list(jax.random.split(key, 40))
    f32 = jnp.float32

    def nk():
        return keys.pop()

    def nrm(shape, scale):
        return jax.random.normal(nk(), shape, f32) * scale

    def gain(shape):
        return 1.0 + nrm(shape, 0.02)

    def a_log(shape):
        return jnp.log(jax.random.uniform(nk(), shape, f32, 1.0, 16.0))

    def dt_bias(shape):
        dt = jnp.exp(jax.random.uniform(nk(), shape, f32, math.log(1e-3), math.log(1e-1)))
        return dt + jnp.log(-jnp.expm1(-dt))

    return {
        'x_prompt': jax.random.normal(nk(), (BATCH, SEQ, D_MODEL), f32),
        'x_sample': jax.random.normal(nk(), (DEC_BATCH, DEC_SEQ, D_MODEL), f32),
        'w_in': nrm((DEPTH, D_MODEL, D_IN), D_MODEL ** -0.5),
        'conv_a_w': nrm((DEPTH, CONV_W, 2 * GDN_QK + GDN_WIDTH), CONV_W ** -0.5),
        'a_A_log': a_log((DEPTH, N_DIR, GDN_HEADS)),
        'a_dt_bias': dt_bias((DEPTH, N_DIR, GDN_HEADS)),
        'a_norm_w': gain((DEPTH, GDN_DV)),
        'conv_b_w': nrm((DEPTH, CONV_W, SSD_WIDTH + 2 * SSD_BC), CONV_W ** -0.5),
        'conv_b_b': nrm((DEPTH, SSD_WIDTH + 2 * SSD_BC), 0.02),
        'b_A_log': a_log((DEPTH, N_DIR, SSD_HEADS)),
        'b_dt_bias': dt_bias((DEPTH, N_DIR, SSD_HEADS)),
        'b_D': gain((DEPTH, SSD_HEADS)),
        'b_norm_w': gain((DEPTH, SSD_WIDTH)),
        'c_i_bias': nrm((DEPTH, N_DIR, MLSTM_HEADS), 0.1),
        'c_f_bias': jnp.linspace(3.0, 6.0, MLSTM_HEADS, dtype=f32) + nrm((DEPTH, N_DIR, MLSTM_HEADS), 0.1),
        'c_norm_w': gain((DEPTH, MLSTM_WIDTH)),
        'w_out': nrm((DEPTH, D_MIX, D_MODEL), DN_BETA * D_MIX ** -0.5),
        'ln1_g': gain((DEPTH, D_MODEL)),
        'ln1_b': nrm((DEPTH, D_MODEL), 0.02),
        'ln2_g': gain((DEPTH, D_MODEL)),
        'ln2_b': nrm((DEPTH, D_MODEL), 0.02),
        'ffn_w_gate': nrm((N_DENSE, D_MODEL, D_FF), D_MODEL ** -0.5),
        'ffn_w_up': nrm((N_DENSE, D_MODEL, D_FF), D_MODEL ** -0.5),
        'ffn_w_down': nrm((N_DENSE, D_FF, D_MODEL), DN_BETA * D_FF ** -0.5),
        'moe_router': nrm((N_MOE, D_MODEL, N_EXPERTS), D_MODEL ** -0.5),
        'moe_w_gate': nrm((N_MOE, N_EXPERTS, D_MODEL, D_EXPERT), D_MODEL ** -0.5),
        'moe_w_up': nrm((N_MOE, N_EXPERTS, D_MODEL, D_EXPERT), D_MODEL ** -0.5),
        'moe_w_down': nrm((N_MOE, N_EXPERTS, D_EXPERT, D_MODEL), DN_BETA * D_EXPERT ** -0.5),
    }


def reference(x_prompt, x_sample, w_in, conv_a_w, a_A_log, a_dt_bias, a_norm_w, conv_b_w, conv_b_b,
              b_A_log, b_dt_bias, b_D, b_norm_w, c_i_bias, c_f_bias, c_norm_w, w_out,
              ln1_g, ln1_b, ln2_g, ln2_b, ffn_w_gate, ffn_w_up, ffn_w_down,
              moe_router, moe_w_gate, moe_w_up, moe_w_down):
    params = (w_in, conv_a_w, a_A_log, a_dt_bias, a_norm_w, conv_b_w, conv_b_b, b_A_log, b_dt_bias,
              b_D, b_norm_w, c_i_bias, c_f_bias, c_norm_w, w_out, ln1_g, ln1_b, ln2_g, ln2_b,
              ffn_w_gate, ffn_w_up, ffn_w_down, moe_router, moe_w_gate, moe_w_up, moe_w_down)
    y_prompt = _trunk(x_prompt, *params)
    y_sample = _trunk(x_sample, *params)
    return (y_prompt, y_sample)
```

```python
import functools

import jax
import jax.numpy as jnp
import numpy as np
from jax import lax
from jax.experimental import pallas as pl
from jax.experimental.pallas import tpu as pltpu

F32 = jnp.float32
BF16 = jnp.bfloat16
HIGHEST = lax.Precision.HIGHEST

D_MODEL = 1024
DEPTH = 2
GDN_HEADS, GDN_DK, GDN_DV = 4, 128, 128
SSD_HEADS, SSD_HEADDIM, SSD_GROUPS, SSD_STATE = 8, 64, 2, 128
MLSTM_HEADS, MLSTM_DH = 4, 128
N_DIR = 2
CONV_W = 4
CONV_PAD_L = (CONV_W - 1) // 2
CHUNK = 64
N_EXPERTS = 8
DN_ALPHA = (2 * DEPTH) ** 0.25
LN_EPS = 1e-5
NORM_EPS = 1e-6
NEG = -1e30

LANE = 128
SUBLANE = 8

HW = 512
COL_AQKV = 0
COL_SXBC = 3 * HW
COL_CQKV = 5 * HW
COL_Z = 8 * HW
COL_GATE = 11 * HW
N_PROJ = COL_GATE + LANE
N_CONV = COL_CQKV

F_BETA, F_G, F_DT, F_ACS, F_IPRE, F_BCUM = 0, 8, 16, 32, 48, 56

TM_PROJ = 256
TB_PREP = 256
TB_MIX = 256
TB_POST = 256
TM_FFN = 512
TM_MOE = 1024
TF_MOE = 512
VMEM_LIMIT = 56 * 1024 * 1024


def _sigmoid(x):
    return 1.0 / (1.0 + jnp.exp(-x))


def _silu(x):
    return x * _sigmoid(x)


def _softplus(x):
    return jnp.maximum(x, 0.0) + jnp.log(1.0 + jnp.exp(-jnp.abs(x)))


def _mm(a, b):
    return jnp.dot(a.astype(BF16), b.astype(BF16), preferred_element_type=F32)


def _mm_nt(a, b):
    return lax.dot_general(a.astype(BF16), b.astype(BF16), (((1,), (1,)), ((), ())),
                           preferred_element_type=F32)


def _mm_tn(a, b):
    return lax.dot_general(a.astype(BF16), b.astype(BF16), (((0,), (0,)), ((), ())),
                           preferred_element_type=F32)


def _layernorm_rows(z, g, b):
    mu = jnp.mean(z, -1, keepdims=True)
    zc = z - mu
    var = jnp.mean(zc * zc, -1, keepdims=True)
    return zc * lax.rsqrt(var + LN_EPS) * g + b


def _inproj_kernel(x_ref, w_ref, o_ref):
    o_ref[...] = jnp.dot(x_ref[...].astype(BF16), w_ref[...], preferred_element_type=F32)


def _in_proj(x, w):
    T, D = x.shape
    N = w.shape[1]
    return pl.pallas_call(
        _inproj_kernel,
        grid=(T // TM_PROJ,),
        in_specs=[pl.BlockSpec((TM_PROJ, D), lambda i: (i, 0)),
                  pl.BlockSpec((D, N), lambda i: (0, 0))],
        out_specs=pl.BlockSpec((TM_PROJ, N), lambda i: (i, 0)),
        out_shape=jax.ShapeDtypeStruct((T, N), F32),
        compiler_params=pltpu.CompilerParams(dimension_semantics=("parallel",),
                                             vmem_limit_bytes=VMEM_LIMIT),
        name="in_proj",
    )(x, w)


def _prep_kernel(xm_ref, xp_ref, xn_ref, g_ref, cw_ref, cb_ref, gp_ref,
                 act_ref, f_ref, ftc_ref, pad_ref, *, tb, nb):
    b = pl.program_id(1)
    has_prev = (b > 0).astype(F32)
    has_next = (b < nb - 1).astype(F32)
    pad_ref[0:SUBLANE, :] = xp_ref[...] * has_prev
    pad_ref[SUBLANE:SUBLANE + tb, :] = xm_ref[...]
    pad_ref[SUBLANE + tb:2 * SUBLANE + tb, :] = xn_ref[...] * has_next

    base = SUBLANE - CONV_PAD_L
    for c0 in range(0, N_CONV, HW):
        y = cb_ref[:, c0:c0 + HW]
        for j in range(CONV_W):
            y = y + pad_ref[base + j:base + j + tb, c0:c0 + HW] * cw_ref[j:j + 1, c0:c0 + HW]
        y = _silu(y)
        if c0 in (COL_AQKV, COL_AQKV + HW):
            scale = GDN_DK ** -0.5 if c0 == COL_AQKV else 1.0
            parts = []
            for h in range(GDN_HEADS):
                seg = y[:, h * GDN_DK:(h + 1) * GDN_DK]
                ss = jnp.sum(seg * seg, -1, keepdims=True)
                parts.append(seg * lax.rsqrt(ss + NORM_EPS) * scale)
            y = jnp.concatenate(parts, axis=1)
        act_ref[:, c0:c0 + HW] = y.astype(BF16)

    p = g_ref[...]
    z = p + gp_ref[0:1, :]
    coef = gp_ref[1:2, :]
    lane = lax.broadcasted_iota(jnp.int32, p.shape, 1)
    sp = _softplus(z)
    val = jnp.where(lane < F_G, _sigmoid(p),
          jnp.where(lane < F_DT, coef * sp,
          jnp.where(lane < F_ACS, sp,
          jnp.where(lane < F_IPRE, coef * sp,
          jnp.where(lane < F_BCUM, z, -_softplus(-z))))))
    is_cum = ((lane >= F_G) & (lane < F_DT)) | ((lane >= F_ACS) & (lane < F_IPRE)) | (
        (lane >= F_BCUM) & (lane < F_BCUM + 8))
    wide = (lane >= F_DT) & (lane < F_IPRE)
    col_bwd = jnp.where(wide, lane % 16, 2 * (lane % 8)) >= 8
    r = lax.broadcasted_iota(jnp.int32, (tb, tb), 0)
    c = lax.broadcasted_iota(jnp.int32, (tb, tb), 1)
    same = (r // CHUNK) == (c // CHUNK)
    tril = jnp.where(same & (c <= r), 1.0, 0.0).astype(F32)
    triu = jnp.where(same & (c >= r), 1.0, 0.0).astype(F32)
    cs_f = jnp.dot(tril, val, precision=HIGHEST, preferred_element_type=F32)
    cs_b = jnp.dot(triu, val, precision=HIGHEST, preferred_element_type=F32)
    feat = jnp.where(is_cum, jnp.where(col_bwd, cs_b, cs_f), val)
    f_ref[...] = feat
    for k in range(tb // CHUNK):
        ftc_ref[k] = feat[k * CHUNK:(k + 1) * CHUNK, :].T


def _prep(proj, conv_w, conv_b, gate_p, ns, seq):
    T = proj.shape[0]
    tb = TB_PREP
    nb = seq // tb
    t8 = tb // SUBLANE
    n8 = T // SUBLANE
    kern = functools.partial(_prep_kernel, tb=tb, nb=nb)
    return pl.pallas_call(
        kern,
        grid=(ns, nb),
        in_specs=[
            pl.BlockSpec((tb, N_CONV), lambda s, b: (s * nb + b, 0)),
            pl.BlockSpec((SUBLANE, N_CONV), lambda s, b: (jnp.maximum((s * nb + b) * t8 - 1, 0), 0)),
            pl.BlockSpec((SUBLANE, N_CONV), lambda s, b: (jnp.minimum((s * nb + b + 1) * t8, n8 - 1), 0)),
            pl.BlockSpec((tb, LANE), lambda s, b: (s * nb + b, COL_GATE // LANE)),
            pl.BlockSpec((SUBLANE, N_CONV), lambda s, b: (0, 0)),
            pl.BlockSpec((1, N_CONV), lambda s, b: (0, 0)),
            pl.BlockSpec((SUBLANE, LANE), lambda s, b: (0, 0)),
        ],
        out_specs=[
            pl.BlockSpec((tb, N_CONV), lambda s, b: (s * nb + b, 0)),
            pl.BlockSpec((tb, LANE), lambda s, b: (s * nb + b, 0)),
            pl.BlockSpec((tb // CHUNK, LANE, CHUNK), lambda s, b: (s * nb + b, 0, 0)),
        ],
        out_shape=[
            jax.ShapeDtypeStruct((T, N_CONV), BF16),
            jax.ShapeDtypeStruct((T, LANE), F32),
            jax.ShapeDtypeStruct((T // CHUNK, LANE, CHUNK), F32),
        ],
        scratch_shapes=[pltpu.VMEM((tb + 2 * SUBLANE, N_CONV), F32)],
        compiler_params=pltpu.CompilerParams(dimension_semantics=("parallel", "parallel"),
                                             vmem_limit_bytes=VMEM_LIMIT),
        name="prep",
    )(proj, proj, proj, proj, conv_w, conv_b, gate_p)


def _expand_features(f_ref, e_ref, targets, d):
    n = e_ref.shape[1]
    crow = lax.broadcasted_iota(jnp.int32, (LANE, n), 0)
    lcol = lax.broadcasted_iota(jnp.int32, (LANE, n), 1)
    tgt = jnp.zeros((LANE, n), jnp.int32)
    off = 0
    for width, base, heads, per in targets:
        seg = base + d * heads + (lcol - off) // per
        tgt = jnp.where((lcol >= off) & (lcol < off + width), seg, tgt)
        off += width
    sel = jnp.where(crow == tgt, 1.0, 0.0).astype(F32)
    e_ref[...] = jnp.dot(f_ref[...], sel, precision=HIGHEST, preferred_element_type=F32)


def _dir_masks(d):
    row = lax.broadcasted_iota(jnp.int32, (CHUNK, CHUNK), 0)
    col = lax.broadcasted_iota(jnp.int32, (CHUNK, CHUNK), 1)
    diff = (row - col) * (1 - 2 * d)
    return row, col, diff >= 0, diff > 0


def _mix_row_map(nb):
    return lambda s, d, i: s * nb + i + d * (nb - 1 - 2 * i)


def _tri_inverse(a, eye, m8, m16, m32, m64):
    a8 = jnp.where(m8, a, 0.0)
    x = eye - a8
    p = _mm(a8, a8)
    x = x + _mm(x, p)
    p = _mm(p, p)
    x = x + _mm(x, p)
    for m in (m16, m32, m64):
        x = x - _mm(x, _mm(jnp.where(m, a, 0.0), x))
    return x


def _gdn_kernel(q_ref, k_ref, v_ref, f_ref, ftc_ref, o_ref, s_ref, e_ref, *, tb):
    d = pl.program_id(1)
    i = pl.program_id(2)

    @pl.when(i == 0)
    def _():
        s_ref[...] = jnp.zeros(s_ref.shape, F32)

    ncb = tb // CHUNK
    _expand_features(f_ref, e_ref, [(HW, F_BETA, GDN_HEADS, GDN_DV), (HW, F_G, GDN_HEADS, GDN_DV)], d)
    row, col, incl, strict = _dir_masks(d)
    eye = jnp.where(row == col, 1.0, 0.0).astype(F32)
    b8 = (row // 8) == (col // 8)
    b16 = (row // 16) == (col // 16)
    b32 = (row // 32) == (col // 32)
    m16 = b16 & jnp.logical_not(b8)
    m32 = b32 & jnp.logical_not(b16)
    m64 = jnp.logical_not(b32)
    tend = (CHUNK - 1) * (1 - d)

    for cc in range(ncb):
        c = cc + d * (ncb - 1 - 2 * cc)
        r0 = pl.multiple_of(c * CHUNK, CHUNK)
        bx = e_ref[pl.ds(r0, CHUNK), 0:HW]
        gx = e_ref[pl.ds(r0, CHUNK), HW:2 * HW]
        gend = e_ref[pl.ds(r0 + tend, 1), HW:2 * HW]
        eg = jnp.exp(gx)
        kdf = jnp.exp(gend - gx)
        ge = jnp.exp(gend)
        q = q_ref[pl.ds(r0, CHUNK), :].astype(F32)
        k = k_ref[pl.ds(r0, CHUNK), :].astype(F32)
        v = v_ref[pl.ds(r0, CHUNK), :].astype(F32)
        outs = []
        for h in range(GDN_HEADS):
            sl = slice(h * GDN_DV, (h + 1) * GDN_DV)
            qh, kh, vh, bh, egh = q[:, sl], k[:, sl], v[:, sl], bx[:, sl], eg[:, sl]
            grow = ftc_ref[c, pl.ds(F_G + d * GDN_HEADS + h, 1), :]
            gd = gx[:, h * GDN_DV:h * GDN_DV + CHUNK] - grow
            decay = jnp.exp(jnp.where(incl, gd, NEG))
            kk = _mm_nt(kh, kh)
            qk = _mm_nt(qh, kh)
            a = jnp.where(strict, bh[:, 0:CHUNK] * kk * decay, 0.0)
            tinv = _tri_inverse(a, eye, b8, m16, m32, m64)
            uw = _mm(tinv, jnp.concatenate([vh * bh, kh * bh * egh], axis=1))
            u, w = uw[:, 0:GDN_DV], uw[:, GDN_DV:2 * GDN_DV]
            s_h = s_ref[:, sl]
            wq = _mm(jnp.concatenate([w, qh * egh], axis=0), s_h)
            v_new = u - wq[0:CHUNK]
            outs.append(wq[CHUNK:2 * CHUNK] + _mm(qk * decay, v_new))
            s_ref[:, sl] = s_h * ge[:, sl] + _mm_tn(kh * kdf[:, sl], v_new)
        o_ref[pl.ds(r0, CHUNK), :] = jnp.concatenate(outs, axis=1)


def _gdn(act, feat, ftc, ns, seq):
    T = act.shape[0]
    tb = TB_MIX
    nb = seq // tb
    rm = _mix_row_map(nb)
    kern = functools.partial(_gdn_kernel, tb=tb)
    qkv = [pl.BlockSpec((tb, HW), lambda s, d, i, j=j: (rm(s, d, i), j)) for j in range(3)]
    return pl.pallas_call(
        kern,
        grid=(ns, N_DIR, nb),
        in_specs=qkv + [
            pl.BlockSpec((tb, LANE), lambda s, d, i: (rm(s, d, i), 0)),
            pl.BlockSpec((tb // CHUNK, LANE, CHUNK), lambda s, d, i: (rm(s, d, i), 0, 0)),
        ],
        out_specs=pl.BlockSpec((None, tb, HW), lambda s, d, i: (d, rm(s, d, i), 0)),
        out_shape=jax.ShapeDtypeStruct((N_DIR, T, HW), F32),
        scratch_shapes=[pltpu.VMEM((GDN_DK, HW), F32), pltpu.VMEM((tb, 2 * HW), F32)],
        compiler_params=pltpu.CompilerParams(
            dimension_semantics=("parallel", "parallel", "arbitrary"), vmem_limit_bytes=VMEM_LIMIT),
        name="gdn_scan",
    )(act, act, act, feat, ftc)


def _ssd_kernel(x_ref, b_ref, c_ref, f_ref, ftc_ref, dp_ref, o_ref, h_ref, e_ref, *, tb):
    d = pl.program_id(1)
    i = pl.program_id(2)

    @pl.when(i == 0)
    def _():
        h_ref[...] = jnp.zeros(h_ref.shape, F32)

    ncb = tb // CHUNK
    _expand_features(f_ref, e_ref, [(HW, F_DT, SSD_HEADS, SSD_HEADDIM), (HW, F_ACS, SSD_HEADS, SSD_HEADDIM),
                                    (SSD_HEADS * LANE, F_ACS, SSD_HEADS, LANE)], d)
    _, _, incl, _ = _dir_masks(d)
    tend = (CHUNK - 1) * (1 - d)
    lane = lax.broadcasted_iota(jnp.int32, (CHUNK, LANE), 1)
    lo = lane < SSD_HEADDIM
    gw = HW // SSD_GROUPS
    epg = SSD_HEADS // SSD_GROUPS
    dfac = (1 - d).astype(F32)

    for cc in range(ncb):
        c = cc + d * (ncb - 1 - 2 * cc)
        r0 = pl.multiple_of(c * CHUNK, CHUNK)
        dtx = e_ref[pl.ds(r0, CHUNK), 0:HW]
        ax = e_ref[pl.ds(r0, CHUNK), HW:2 * HW]
        aend = e_ref[pl.ds(r0 + tend, 1), HW:2 * HW]
        x = x_ref[pl.ds(r0, CHUNK), :].astype(F32)
        xdt = x * dtx
        xdte = xdt * jnp.exp(aend - ax)
        ea = jnp.exp(ax)
        cd = jnp.exp(aend)
        bm = b_ref[pl.ds(r0, CHUNK), :]
        cm = c_ref[pl.ds(r0, CHUNK), :]
        outs = []
        for g in range(SSD_GROUPS):
            bg = bm[:, g * SSD_STATE:(g + 1) * SSD_STATE]
            cg = cm[:, g * SSD_STATE:(g + 1) * SSD_STATE]
            cb = _mm_nt(cg, bg)
            gl = slice(g * gw, (g + 1) * gw)
            h_g = h_ref[:, gl]
            y_g = _mm(cg, h_g) * ea[:, gl]
            yd = []
            for pr in range(epg // 2):
                ps = slice(g * gw + pr * LANE, g * gw + (pr + 1) * LANE)
                xp = xdt[:, ps]
                acc = None
                for half in range(2):
                    hh = g * epg + pr * 2 + half
                    arow = ftc_ref[c, pl.ds(F_ACS + d * SSD_HEADS + hh, 1), :]
                    acol = e_ref[pl.ds(r0, CHUNK), 2 * HW + hh * LANE:2 * HW + hh * LANE + CHUNK]
                    lmat = jnp.exp(jnp.where(incl, acol - arow, NEG))
                    xh = jnp.where(lo, xp, 0.0) if half == 0 else jnp.where(lo, 0.0, xp)
                    t = _mm(cb * lmat, xh)
                    acc = t if acc is None else acc + t
                yd.append(acc)
            outs.append(y_g + jnp.concatenate(yd, axis=1))
            h_ref[:, gl] = h_g * cd[:, gl] + _mm_tn(bg, xdte[:, gl])
        y = jnp.concatenate(outs, axis=1) + x * (dp_ref[...] * dfac)
        o_ref[pl.ds(r0, CHUNK), :] = y


def _ssd(act, feat, ftc, d_row, ns, seq):
    T = act.shape[0]
    tb = TB_MIX
    nb = seq // tb
    rm = _mix_row_map(nb)
    kern = functools.partial(_ssd_kernel, tb=tb)
    bc_w = SSD_GROUPS * SSD_STATE
    return pl.pallas_call(
        kern,
        grid=(ns, N_DIR, nb),
        in_specs=[
            pl.BlockSpec((tb, HW), lambda s, d, i: (rm(s, d, i), COL_SXBC // HW)),
            pl.BlockSpec((tb, bc_w), lambda s, d, i: (rm(s, d, i), (COL_SXBC + HW) // bc_w)),
            pl.BlockSpec((tb, bc_w), lambda s, d, i: (rm(s, d, i), (COL_SXBC + HW) // bc_w + 1)),
            pl.BlockSpec((tb, LANE), lambda s, d, i: (rm(s, d, i), 0)),
            pl.BlockSpec((tb // CHUNK, LANE, CHUNK), lambda s, d, i: (rm(s, d, i), 0, 0)),
            pl.BlockSpec((1, HW), lambda s, d, i: (0, 0)),
        ],
        out_specs=pl.BlockSpec((None, tb, HW), lambda s, d, i: (d, rm(s, d, i), 0)),
        out_shape=jax.ShapeDtypeStruct((N_DIR, T, HW), F32),
        scratch_shapes=[pltpu.VMEM((SSD_STATE, HW), F32), pltpu.VMEM((tb, 2 * HW + SSD_HEADS * LANE), F32)],
        compiler_params=pltpu.CompilerParams(
            dimension_semantics=("parallel", "parallel", "arbitrary"), vmem_limit_bytes=VMEM_LIMIT),
        name="ssd_scan",
    )(act, act, act, feat, ftc, d_row)


def _mlstm_kernel(q_ref, k_ref, v_ref, f_ref, ftc_ref, o_ref, c_ref, m_ref, e_ref, *, tb):
    d = pl.program_id(1)
    i = pl.program_id(2)

    @pl.when(i == 0)
    def _():
        c_ref[...] = jnp.zeros(c_ref.shape, F32)
        m_ref[...] = jnp.zeros(m_ref.shape, F32)

    ncb = tb // CHUNK
    dh = MLSTM_DH
    _expand_features(f_ref, e_ref, [(HW, F_IPRE, MLSTM_HEADS, dh), (HW, F_BCUM, MLSTM_HEADS, dh)], d)
    _, _, incl, _ = _dir_masks(d)
    tend = (CHUNK - 1) * (1 - d)
    ones = jnp.ones((CHUNK, dh), F32)

    for cc in range(ncb):
        c = cc + d * (ncb - 1 - 2 * cc)
        r0 = pl.multiple_of(c * CHUNK, CHUNK)
        ix = e_ref[pl.ds(r0, CHUNK), 0:HW]
        bx = e_ref[pl.ds(r0, CHUNK), HW:2 * HW]
        bend = e_ref[pl.ds(r0 + tend, 1), HW:2 * HW]
        logw = bend - bx + ix
        mc = jnp.max(logw, axis=0, keepdims=True)
        ws = jnp.exp(logw - mc)
        m_old = m_ref[...]
        m_new = jnp.maximum(bend + m_old, mc)
        s_old = jnp.exp(bend + m_old - m_new)
        s_add = jnp.exp(mc - m_new)
        m_inter = bx + m_old
        q = q_ref[pl.ds(r0, CHUNK), :]
        k = k_ref[pl.ds(r0, CHUNK), :] * (dh ** -0.5)
        v = v_ref[pl.ds(r0, CHUNK), :]
        outs = []
        for h in range(MLSTM_HEADS):
            sl = slice(h * dh, (h + 1) * dh)
            al = slice(h * 2 * dh, (h + 1) * 2 * dh)
            qh, kh, vh = q[:, sl], k[:, sl], v[:, sl]
            brow = ftc_ref[c, pl.ds(F_BCUM + d * MLSTM_HEADS + h, 1), :]
            irow = ftc_ref[c, pl.ds(F_IPRE + d * MLSTM_HEADS + h, 1), :]
            logd = jnp.where(incl, bx[:, h * dh:h * dh + CHUNK] - brow + irow, NEG)
            mi = jnp.max(logd, axis=-1, keepdims=True)
            qkd = _mm_nt(qh, kh) * jnp.exp(logd - mi)
            vaug = jnp.concatenate([vh, ones], axis=1)
            intra = _mm(qkd, vaug)
            c_h = c_ref[:, al]
            inter = _mm(qh, c_h)
            mint = m_inter[:, sl]
            m_t = jnp.maximum(mint, mi)
            s_inter = jnp.exp(mint - m_t)
            s_intra = jnp.exp(mi - m_t)
            num = s_inter * inter[:, 0:dh] + s_intra * intra[:, 0:dh]
            den = s_inter * inter[:, dh:2 * dh] + s_intra * intra[:, dh:2 * dh]
            outs.append(num / jnp.maximum(jnp.abs(den), jnp.exp(-m_t)))
            so = jnp.concatenate([s_old[:, sl], s_old[:, sl]], axis=1)
            sa = jnp.concatenate([s_add[:, sl], s_add[:, sl]], axis=1)
            c_ref[:, al] = c_h * so + _mm_tn(kh * ws[:, sl], vaug) * sa
        m_ref[...] = m_new
        o_ref[pl.ds(r0, CHUNK), :] = jnp.concatenate(outs, axis=1)


def _mlstm(proj, feat, ftc, ns, seq):
    T = proj.shape[0]
    tb = TB_MIX
    nb = seq // tb
    rm = _mix_row_map(nb)
    kern = functools.partial(_mlstm_kernel, tb=tb)
    qkv = [pl.BlockSpec((tb, HW), lambda s, d, i, j=j: (rm(s, d, i), COL_CQKV // HW + j)) for j in range(3)]
    return pl.pallas_call(
        kern,
        grid=(ns, N_DIR, nb),
        in_specs=qkv + [
            pl.BlockSpec((tb, LANE), lambda s, d, i: (rm(s, d, i), 0)),
            pl.BlockSpec((tb // CHUNK, LANE, CHUNK), lambda s, d, i: (rm(s, d, i), 0, 0)),
        ],
        out_specs=pl.BlockSpec((None, tb, HW), lambda s, d, i: (d, rm(s, d, i), 0)),
        out_shape=jax.ShapeDtypeStruct((N_DIR, T, HW), F32),
        scratch_shapes=[pltpu.VMEM((MLSTM_DH, 2 * HW), F32), pltpu.VMEM((1, HW), F32),
                        pltpu.VMEM((tb, 2 * HW), F32)],
        compiler_params=pltpu.CompilerParams(
            dimension_semantics=("parallel", "parallel", "arbitrary"), vmem_limit_bytes=VMEM_LIMIT),
        name="mlstm_scan",
    )(proj, proj, proj, feat, ftc)


def _post_kernel(og_ref, ys_ref, hm_ref, az_ref, sz_ref, co_ref, x_ref, wo_ref, nw_ref, ln_ref, o_ref):
    oa = og_ref[0] + og_ref[1]
    az = az_ref[...]
    parts = []
    for h in range(GDN_HEADS):
        sl = slice(h * GDN_DV, (h + 1) * GDN_DV)
        seg = oa[:, sl]
        ms = jnp.mean(seg * seg, -1, keepdims=True)
        parts.append(seg * lax.rsqrt(ms + NORM_EPS) * nw_ref[0:1, sl] * _silu(az[:, sl]))
    ys = (ys_ref[0] + ys_ref[1]) * _silu(sz_ref[...])
    gw = HW // SSD_GROUPS
    for g in range(SSD_GROUPS):
        sl = slice(g * gw, (g + 1) * gw)
        seg = ys[:, sl]
        ms = jnp.mean(seg * seg, -1, keepdims=True)
        parts.append(seg * lax.rsqrt(ms + NORM_EPS) * nw_ref[1:2, sl])
    hc = hm_ref[0] + hm_ref[1]
    co = co_ref[...]
    for h in range(MLSTM_HEADS):
        sl = slice(h * MLSTM_DH, (h + 1) * MLSTM_DH)
        seg = hc[:, sl]
        seg = seg - jnp.mean(seg, -1, keepdims=True)
        ms = jnp.mean(seg * seg, -1, keepdims=True)
        parts.append(_sigmoid(co[:, sl]) * (seg * lax.rsqrt(ms + NORM_EPS) * nw_ref[2:3, sl]))
    y = jnp.concatenate(parts, axis=1).astype(BF16)
    hmix = jnp.dot(y, wo_ref[...], preferred_element_type=F32)
    z = DN_ALPHA * x_ref[...] + hmix
    o_ref[...] = _layernorm_rows(z, ln_ref[0:1, :], ln_ref[1:2, :])


def _post(og, ys, hm, proj, x, w_out, norm_w, ln):
    T, D = x.shape
    tb = TB_POST
    dir_spec = pl.BlockSpec((N_DIR, tb, HW), lambda i: (0, i, 0))
    zc = COL_Z // HW
    return pl.pallas_call(
        _post_kernel,
        grid=(T // tb,),
        in_specs=[dir_spec, dir_spec, dir_spec,
                  pl.BlockSpec((tb, HW), lambda i: (i, zc)),
                  pl.BlockSpec((tb, HW), lambda i: (i, zc + 1)),
                  pl.BlockSpec((tb, HW), lambda i: (i, zc + 2)),
                  pl.BlockSpec((tb, D), lambda i: (i, 0)),
                  pl.BlockSpec(w_out.shape, lambda i: (0, 0)),
                  pl.BlockSpec((SUBLANE, HW), lambda i: (0, 0)),
                  pl.BlockSpec((SUBLANE, D), lambda i: (0, 0))],
        out_specs=pl.BlockSpec((tb, D), lambda i: (i, 0)),
        out_shape=jax.ShapeDtypeStruct((T, D), F32),
        compiler_params=pltpu.CompilerParams(dimension_semantics=("parallel",),
                                             vmem_limit_bytes=VMEM_LIMIT),
        name="post",
    )(og, ys, hm, proj, proj, proj, x, w_out, norm_w, ln)


def _ffn_kernel(x_ref, wg_ref, wu_ref, wd_ref, ln_ref, o_ref, acc_ref, *, nf):
    f = pl.program_id(1)
    xb = x_ref[...].astype(BF16)
    g = jnp.dot(xb, wg_ref[...], preferred_element_type=F32)
    u = jnp.dot(xb, wu_ref[...], preferred_element_type=F32)
    contrib = jnp.dot((_silu(g) * u).astype(BF16), wd_ref[...], preferred_element_type=F32)

    @pl.when(f == 0)
    def _():
        acc_ref[...] = contrib

    @pl.when(f > 0)
    def _():
        acc_ref[...] += contrib

    @pl.when(f == nf - 1)
    def _():
        z = DN_ALPHA * x_ref[...] + acc_ref[...]
        o_ref[...] = _layernorm_rows(z, ln_ref[0:1, :], ln_ref[1:2, :])


def _ffn(x, wg, wu, wd, ln):
    T, D = x.shape
    F = wg.shape[1]
    tm = TM_FFN
    tf = F // 2
    nf = F // tf
    kern = functools.partial(_ffn_kernel, nf=nf)
    return pl.pallas_call(
        kern,
        grid=(T // tm, nf),
        in_specs=[pl.BlockSpec((tm, D), lambda i, f: (i, 0)),
                  pl.BlockSpec((D, tf), lambda i, f: (0, f)),
                  pl.BlockSpec((D, tf), lambda i, f: (0, f)),
                  pl.BlockSpec((tf, D), lambda i, f: (f, 0)),
                  pl.BlockSpec((SUBLANE, D), lambda i, f: (0, 0))],
        out_specs=pl.BlockSpec((tm, D), lambda i, f: (i, 0)),
        out_shape=jax.ShapeDtypeStruct((T, D), F32),
        scratch_shapes=[pltpu.VMEM((tm, D), F32)],
        compiler_params=pltpu.CompilerParams(dimension_semantics=("parallel", "arbitrary"),
                                             vmem_limit_bytes=VMEM_LIMIT),
        name="ffn",
    )(x, wg, wu, wd, ln)


def _moe_kernel(x_ref, r_ref, wg_ref, wu_ref, wd_ref, ln_ref, o_ref, acc_ref, comb_ref, *, ne, nf):
    e = pl.program_id(1)
    f = pl.program_id(2)
    first = (e == 0) & (f == 0)

    @pl.when(first)
    def _():
        logits = jnp.dot(x_ref[...], r_ref[...], precision=HIGHEST, preferred_element_type=F32)
        lane = lax.broadcasted_iota(jnp.int32, logits.shape, 1)
        lanef = lane.astype(F32)
        logits = jnp.where(lane < ne, logits, NEG)
        m1 = jnp.max(logits, -1, keepdims=True)
        i1 = jnp.min(jnp.where(logits == m1, lanef, float(LANE)), -1, keepdims=True)
        oh1 = lanef == i1
        rest = jnp.where(oh1, NEG, logits)
        m2 = jnp.max(rest, -1, keepdims=True)
        i2 = jnp.min(jnp.where(rest == m2, lanef, float(LANE)), -1, keepdims=True)
        oh2 = lanef == i2
        e2 = jnp.exp(m2 - m1)
        comb_ref[...] = jnp.where(oh1, 1.0 / (1.0 + e2), 0.0) + jnp.where(oh2, e2 / (1.0 + e2), 0.0)

    comb = comb_ref[...]
    lane = lax.broadcasted_iota(jnp.int32, comb.shape, 1)
    cw = jnp.sum(jnp.where(lane == e, comb, 0.0), -1, keepdims=True)
    xb = x_ref[...].astype(BF16)
    g = jnp.dot(xb, wg_ref[...], preferred_element_type=F32)
    u = jnp.dot(xb, wu_ref[...], preferred_element_type=F32)
    contrib = jnp.dot((_silu(g) * u * cw).astype(BF16), wd_ref[...], preferred_element_type=F32)

    @pl.when(first)
    def _():
        acc_ref[...] = contrib

    @pl.when(jnp.logical_not(first))
    def _():
        acc_ref[...] += contrib

    @pl.when((e == ne - 1) & (f == nf - 1))
    def _():
        z = DN_ALPHA * x_ref[...] + acc_ref[...]
        o_ref[...] = _layernorm_rows(z, ln_ref[0:1, :], ln_ref[1:2, :])


def _moe(x, router, wg, wu, wd, ln):
    T, D = x.shape
    ne, _, F = wg.shape
    tm = TM_MOE
    tf = TF_MOE
    nf = F // tf
    kern = functools.partial(_moe_kernel, ne=ne, nf=nf)
    return pl.pallas_call(
        kern,
        grid=(T // tm, ne, nf),
        in_specs=[pl.BlockSpec((tm, D), lambda i, e, f: (i, 0)),
                  pl.BlockSpec((D, LANE), lambda i, e, f: (0, 0)),
                  pl.BlockSpec((None, D, tf), lambda i, e, f: (e, 0, f)),
                  pl.BlockSpec((None, D, tf), lambda i, e, f: (e, 0, f)),
                  pl.BlockSpec((None, tf, D), lambda i, e, f: (e, f, 0)),
                  pl.BlockSpec((SUBLANE, D), lambda i, e, f: (0, 0))],
        out_specs=pl.BlockSpec((tm, D), lambda i, e, f: (i, 0)),
        out_shape=jax.ShapeDtypeStruct((T, D), F32),
        scratch_shapes=[pltpu.VMEM((tm, D), F32), pltpu.VMEM((tm, LANE), F32)],
        compiler_params=pltpu.CompilerParams(
            dimension_semantics=("parallel", "arbitrary", "arbitrary"), vmem_limit_bytes=VMEM_LIMIT),
        name="moe",
    )(x, router, wg, wu, wd, ln)


def _pad_rows(a, rows):
    return jnp.concatenate([a, jnp.zeros((rows - a.shape[0],) + a.shape[1:], a.dtype)], axis=0)


def _pack_w_in(w):
    gq = GDN_HEADS * GDN_DK
    gv = GDN_HEADS * GDN_DV
    sw = SSD_HEADS * SSD_HEADDIM
    sbc = SSD_GROUPS * SSD_STATE
    mw = MLSTM_HEADS * MLSTM_DH
    sizes = (gq, gq, gv, gv, N_DIR * GDN_HEADS, N_DIR * GDN_HEADS,
             sw, sw, sbc, sbc, N_DIR * SSD_HEADS,
             mw, mw, mw, mw, N_DIR * MLSTM_HEADS, N_DIR * MLSTM_HEADS)
    idx = np.cumsum(sizes)[:-1].tolist()
    (a_q, a_k, a_v, a_z, a_b, a_a, s_z, s_x, s_b, s_c, s_dt,
     c_q, c_k, c_v, c_o, c_i, c_f) = jnp.split(w, idx, axis=1)
    gates = jnp.concatenate([a_b, a_a, s_dt, s_dt, c_i, c_f], axis=1)
    gates = jnp.concatenate([gates, jnp.zeros((w.shape[0], LANE - gates.shape[1]), w.dtype)], axis=1)
    packed = jnp.concatenate([a_q, a_k, a_v, s_x, s_b, s_c, c_q, c_k, c_v, a_z, s_z, c_o, gates], axis=1)
    return packed.astype(BF16)


def _pack_gate_params(a_A_log, a_dt_bias, b_A_log, b_dt_bias, c_i_bias, c_f_bias):
    z8 = jnp.zeros((N_DIR * GDN_HEADS,), F32)
    z16 = jnp.zeros((N_DIR * SSD_HEADS,), F32)
    bias = jnp.concatenate([z8, a_dt_bias.reshape(-1), b_dt_bias.reshape(-1), b_dt_bias.reshape(-1),
                            c_i_bias.reshape(-1), c_f_bias.reshape(-1)])
    coef = jnp.concatenate([z8, -jnp.exp(a_A_log.reshape(-1)), z16, -jnp.exp(b_A_log.reshape(-1)), z8, z8])
    rows = jnp.stack([bias, coef]).astype(F32)
    rows = jnp.concatenate([rows, jnp.zeros((2, LANE - rows.shape[1]), F32)], axis=1)
    return _pad_rows(rows, SUBLANE)


def _mixer_layer(x, ns, seq, w_in, conv_a_w, a_A_log, a_dt_bias, a_norm_w, conv_b_w, conv_b_b,
                 b_A_log, b_dt_bias, b_D, b_norm_w, c_i_bias, c_f_bias, c_norm_w, w_out, ln_g, ln_b):
    proj = _in_proj(x, _pack_w_in(w_in))
    conv_w = _pad_rows(jnp.concatenate([conv_a_w, conv_b_w], axis=1).astype(F32), SUBLANE)
    conv_b = jnp.concatenate([jnp.zeros((conv_a_w.shape[1],), F32), conv_b_b.astype(F32)])[None, :]
    gate_p = _pack_gate_params(a_A_log, a_dt_bias, b_A_log, b_dt_bias, c_i_bias, c_f_bias)
    act, feat, ftc = _prep(proj, conv_w, conv_b, gate_p, ns, seq)
    og = _gdn(act, feat, ftc, ns, seq)
    d_row = jnp.repeat(b_D.astype(F32), SSD_HEADDIM)[None, :]
    ys = _ssd(act, feat, ftc, d_row, ns, seq)
    hm = _mlstm(proj, feat, ftc, ns, seq)
    norm_w = _pad_rows(jnp.stack([jnp.tile(a_norm_w, GDN_HEADS), b_norm_w, c_norm_w]).astype(F32), SUBLANE)
    ln = _pad_rows(jnp.stack([ln_g, ln_b]).astype(F32), SUBLANE)
    return _post(og, ys, hm, proj, x, w_out.astype(BF16), norm_w, ln)


def kernel(x_prompt, x_sample, w_in, conv_a_w, a_A_log, a_dt_bias, a_norm_w, conv_b_w, conv_b_b, b_A_log, b_dt_bias, b_D, b_norm_w, c_i_bias, c_f_bias, c_norm_w, w_out, ln1_g, ln1_b, ln2_g, ln2_b, ffn_w_gate, ffn_w_up, ffn_w_down, moe_router, moe_w_gate, moe_w_up, moe_w_down):
    bp, seq, dm = x_prompt.shape
    bs, seq_s, _ = x_sample.shape
    assert seq == seq_s and dm == D_MODEL and w_in.shape[0] == DEPTH
    assert seq % TB_MIX == 0 and seq % TB_PREP == 0
    ns = bp + bs
    x = jnp.concatenate([x_prompt.reshape(bp * seq, dm), x_sample.reshape(bs * seq, dm)], axis=0)
    for l in range(DEPTH):
        x = _mixer_layer(x, ns, seq, w_in[l], conv_a_w[l], a_A_log[l], a_dt_bias[l], a_norm_w[l],
                         conv_b_w[l], conv_b_b[l], b_A_log[l], b_dt_bias[l], b_D[l], b_norm_w[l],
                         c_i_bias[l], c_f_bias[l], c_norm_w[l], w_out[l], ln1_g[l], ln1_b[l])
        ln2 = _pad_rows(jnp.stack([ln2_g[l], ln2_b[l]]).astype(F32), SUBLANE)
        j = l // 2
        if l % 2 == 0:
            x = _ffn(x, ffn_w_gate[j].astype(BF16), ffn_w_up[j].astype(BF16), ffn_w_down[j].astype(BF16), ln2)
        else:
            router = jnp.concatenate(
                [moe_router[j], jnp.zeros((dm, LANE - N_EXPERTS), moe_router.dtype)], axis=1).astype(F32)
            x = _moe(x, router, moe_w_gate[j].astype(BF16), moe_w_up[j].astype(BF16),
                     moe_w_down[j].astype(BF16), ln2)
    y_prompt = x[:bp * seq].reshape(bp, seq, dm)
    y_sample = x[bp * seq:].reshape(bs, seq, dm)
    return (y_prompt, y_sample)
```

```python
import functools

import jax
import jax.numpy as jnp
import numpy as np
from jax import lax
from jax.experimental import pallas as pl
from jax.experimental.pallas import tpu as pltpu

F32 = jnp.float32
BF16 = jnp.bfloat16
HIGHEST = lax.Precision.HIGHEST

D_MODEL = 1024
DEPTH = 2
GDN_HEADS, GDN_DK, GDN_DV = 4, 128, 128
SSD_HEADS, SSD_HEADDIM, SSD_GROUPS, SSD_STATE = 8, 64, 2, 128
MLSTM_HEADS, MLSTM_DH = 4, 128
N_DIR = 2
CONV_W = 4
CONV_PAD_L = (CONV_W - 1) // 2
CHUNK = 64
N_EXPERTS = 8
DN_ALPHA = (2 * DEPTH) ** 0.25
LN_EPS = 1e-5
NORM_EPS = 1e-6
NEG = -1e30

LANE = 128
SUBLANE = 8

HW = 512
COL_AQKV = 0
COL_SXBC = 3 * HW
COL_CQKV = 5 * HW
COL_Z = 8 * HW
COL_GATE = 11 * HW
N_PROJ = COL_GATE + LANE
N_CONV = COL_CQKV

F_BETA, F_G, F_DT, F_ACS, F_IPRE, F_BCUM = 0, 8, 16, 32, 48, 56

TM_PROJ = 256
TB_PREP = 256
TB_MIX = 512
TB_POST = 256
TM_FFN = 512
TM_ROUTE = 512
TM_GFFN = 256
TF_MOE = 512
N_TOP = 2
VMEM_LIMIT = 56 * 1024 * 1024


def _sigmoid(x):
    return 1.0 / (1.0 + jnp.exp(-x))


def _silu(x):
    return x * _sigmoid(x)


def _softplus(x):
    return jnp.maximum(x, 0.0) + jnp.log(1.0 + jnp.exp(-jnp.abs(x)))


def _mm(a, b):
    return jnp.dot(a.astype(BF16), b.astype(BF16), preferred_element_type=F32)


def _mm_nt(a, b):
    return lax.dot_general(a.astype(BF16), b.astype(BF16), (((1,), (1,)), ((), ())),
                           preferred_element_type=F32)


def _mm_tn(a, b):
    return lax.dot_general(a.astype(BF16), b.astype(BF16), (((0,), (0,)), ((), ())),
                           preferred_element_type=F32)


def _layernorm_rows(z, g, b):
    mu = jnp.mean(z, -1, keepdims=True)
    zc = z - mu
    var = jnp.mean(zc * zc, -1, keepdims=True)
    return zc * lax.rsqrt(var + LN_EPS) * g + b


def _inproj_kernel(x_ref, w_ref, o_ref):
    o_ref[...] = jnp.dot(x_ref[...].astype(BF16), w_ref[...], preferred_element_type=F32)


def _in_proj(x, w):
    T, D = x.shape
    N = w.shape[1]
    return pl.pallas_call(
        _inproj_kernel,
        grid=(T // TM_PROJ,),
        in_specs=[pl.BlockSpec((TM_PROJ, D), lambda i: (i, 0)),
                  pl.BlockSpec((D, N), lambda i: (0, 0))],
        out_specs=pl.BlockSpec((TM_PROJ, N), lambda i: (i, 0)),
        out_shape=jax.ShapeDtypeStruct((T, N), F32),
        compiler_params=pltpu.CompilerParams(dimension_semantics=("parallel",),
                                             vmem_limit_bytes=VMEM_LIMIT),
        name="in_proj",
    )(x, w)


def _prep_kernel(xm_ref, xp_ref, xn_ref, g_ref, cw_ref, cb_ref, gp_ref,
                 act_ref, f_ref, ftc_ref, pad_ref, *, tb, nb):
    b = pl.program_id(1)
    has_prev = (b > 0).astype(F32)
    has_next = (b < nb - 1).astype(F32)
    pad_ref[0:SUBLANE, :] = xp_ref[...] * has_prev
    pad_ref[SUBLANE:SUBLANE + tb, :] = xm_ref[...]
    pad_ref[SUBLANE + tb:2 * SUBLANE + tb, :] = xn_ref[...] * has_next

    base = SUBLANE - CONV_PAD_L
    for c0 in range(0, N_CONV, HW):
        y = cb_ref[:, c0:c0 + HW]
        for j in range(CONV_W):
            y = y + pad_ref[base + j:base + j + tb, c0:c0 + HW] * cw_ref[j:j + 1, c0:c0 + HW]
        y = _silu(y)
        if c0 in (COL_AQKV, COL_AQKV + HW):
            scale = GDN_DK ** -0.5 if c0 == COL_AQKV else 1.0
            parts = []
            for h in range(GDN_HEADS):
                seg = y[:, h * GDN_DK:(h + 1) * GDN_DK]
                ss = jnp.sum(seg * seg, -1, keepdims=True)
                parts.append(seg * lax.rsqrt(ss + NORM_EPS) * scale)
            y = jnp.concatenate(parts, axis=1)
        act_ref[:, c0:c0 + HW] = y.astype(BF16)

    p = g_ref[...]
    z = p + gp_ref[0:1, :]
    coef = gp_ref[1:2, :]
    lane = lax.broadcasted_iota(jnp.int32, p.shape, 1)
    sp = _softplus(z)
    val = jnp.where(lane < F_G, _sigmoid(p),
          jnp.where(lane < F_DT, coef * sp,
          jnp.where(lane < F_ACS, sp,
          jnp.where(lane < F_IPRE, coef * sp,
          jnp.where(lane < F_BCUM, z, -_softplus(-z))))))
    is_cum = ((lane >= F_G) & (lane < F_DT)) | ((lane >= F_ACS) & (lane < F_IPRE)) | (
        (lane >= F_BCUM) & (lane < F_BCUM + 8))
    wide = (lane >= F_DT) & (lane < F_IPRE)
    col_bwd = jnp.where(wide, lane % 16, 2 * (lane % 8)) >= 8
    r = lax.broadcasted_iota(jnp.int32, (tb, tb), 0)
    c = lax.broadcasted_iota(jnp.int32, (tb, tb), 1)
    same = (r // CHUNK) == (c // CHUNK)
    tril = jnp.where(same & (c <= r), 1.0, 0.0).astype(F32)
    triu = jnp.where(same & (c >= r), 1.0, 0.0).astype(F32)
    cs_f = jnp.dot(tril, val, precision=HIGHEST, preferred_element_type=F32)
    cs_b = jnp.dot(triu, val, precision=HIGHEST, preferred_element_type=F32)
    feat = jnp.where(is_cum, jnp.where(col_bwd, cs_b, cs_f), val)
    f_ref[...] = feat
    for k in range(tb // CHUNK):
        ftc_ref[k] = feat[k * CHUNK:(k + 1) * CHUNK, :].T


def _prep(proj, conv_w, conv_b, gate_p, ns, seq):
    T = proj.shape[0]
    tb = TB_PREP
    nb = seq // tb
    t8 = tb // SUBLANE
    n8 = T // SUBLANE
    kern = functools.partial(_prep_kernel, tb=tb, nb=nb)
    return pl.pallas_call(
        kern,
        grid=(ns, nb),
        in_specs=[
            pl.BlockSpec((tb, N_CONV), lambda s, b: (s * nb + b, 0)),
            pl.BlockSpec((SUBLANE, N_CONV), lambda s, b: (jnp.maximum((s * nb + b) * t8 - 1, 0), 0)),
            pl.BlockSpec((SUBLANE, N_CONV), lambda s, b: (jnp.minimum((s * nb + b + 1) * t8, n8 - 1), 0)),
            pl.BlockSpec((tb, LANE), lambda s, b: (s * nb + b, COL_GATE // LANE)),
            pl.BlockSpec((SUBLANE, N_CONV), lambda s, b: (0, 0)),
            pl.BlockSpec((1, N_CONV), lambda s, b: (0, 0)),
            pl.BlockSpec((SUBLANE, LANE), lambda s, b: (0, 0)),
        ],
        out_specs=[
            pl.BlockSpec((tb, N_CONV), lambda s, b: (s * nb + b, 0)),
            pl.BlockSpec((tb, LANE), lambda s, b: (s * nb + b, 0)),
            pl.BlockSpec((tb // CHUNK, LANE, CHUNK), lambda s, b: (s * nb + b, 0, 0)),
        ],
        out_shape=[
            jax.ShapeDtypeStruct((T, N_CONV), BF16),
            jax.ShapeDtypeStruct((T, LANE), F32),
            jax.ShapeDtypeStruct((T // CHUNK, LANE, CHUNK), F32),
        ],
        scratch_shapes=[pltpu.VMEM((tb + 2 * SUBLANE, N_CONV), F32)],
        compiler_params=pltpu.CompilerParams(dimension_semantics=("parallel", "parallel"),
                                             vmem_limit_bytes=VMEM_LIMIT),
        name="prep",
    )(proj, proj, proj, proj, conv_w, conv_b, gate_p)


def _expand_features(f_ref, e_ref, targets, d):
    n = e_ref.shape[1]
    crow = lax.broadcasted_iota(jnp.int32, (LANE, n), 0)
    lcol = lax.broadcasted_iota(jnp.int32, (LANE, n), 1)
    tgt = jnp.zeros((LANE, n), jnp.int32)
    off = 0
    for width, base, heads, per in targets:
        seg = base + d * heads + (lcol - off) // per
        tgt = jnp.where((lcol >= off) & (lcol < off + width), seg, tgt)
        off += width
    sel = jnp.where(crow == tgt, 1.0, 0.0).astype(BF16)
    f = f_ref[...]
    hi = f.astype(BF16)
    r1 = f - hi.astype(F32)
    mid = r1.astype(BF16)
    lo = (r1 - mid.astype(F32)).astype(BF16)
    e_ref[...] = (jnp.dot(jnp.concatenate([hi, mid], axis=1), jnp.concatenate([sel, sel], axis=0),
                          preferred_element_type=F32)
                  + jnp.dot(lo, sel, preferred_element_type=F32))


def _dir_masks(d):
    row = lax.broadcasted_iota(jnp.int32, (CHUNK, CHUNK), 0)
    col = lax.broadcasted_iota(jnp.int32, (CHUNK, CHUNK), 1)
    diff = (row - col) * (1 - 2 * d)
    return row, col, diff >= 0, diff > 0


def _mix_row_map(nb):
    return lambda s, d, i: s * nb + i + d * (nb - 1 - 2 * i)


def _tri_inverse_all(a_list, eye, m8, m16, m32, m64):
    a8 = [jnp.where(m8, a, 0.0) for a in a_list]
    x = [eye - a for a in a8]
    a8 = [a.astype(BF16) for a in a8]
    p2 = [_mm(a, a).astype(BF16) for a in a8]
    x = [xi + _mm(xi, pi) for xi, pi in zip(x, p2)]
    p4 = [_mm(pi, pi) for pi in p2]
    x = [xi + _mm(xi, pi) for xi, pi in zip(x, p4)]
    for m in (m16, m32, m64):
        xb = [t.astype(BF16) for t in x]
        y = [_mm(jnp.where(m, a, 0.0), xi) for a, xi in zip(a_list, xb)]
        x = [xf - _mm(xi, yi) for xf, xi, yi in zip(x, xb, y)]
    return x


def _gdn_kernel(q_ref, k_ref, v_ref, f_ref, ftc_ref, o_ref, s_ref, e_ref, *, tb):
    d = pl.program_id(1)
    i = pl.program_id(2)

    @pl.when(i == 0)
    def _():
        s_ref[...] = jnp.zeros(s_ref.shape, F32)

    ncb = tb // CHUNK
    _expand_features(f_ref, e_ref, [(HW, F_BETA, GDN_HEADS, GDN_DV), (HW, F_G, GDN_HEADS, GDN_DV)], d)
    row, col, incl, strict = _dir_masks(d)
    eye = jnp.where(row == col, 1.0, 0.0).astype(F32)
    b8 = (row // 8) == (col // 8)
    b16 = (row // 16) == (col // 16)
    b32 = (row // 32) == (col // 32)
    m16 = b16 & jnp.logical_not(b8)
    m32 = b32 & jnp.logical_not(b16)
    m64 = jnp.logical_not(b32)
    tend = (CHUNK - 1) * (1 - d)

    heads = range(GDN_HEADS)
    rows, ges = [], []
    qb, kb, rhs, qde, kde, bcol, dec = [], [], [], [], [], [], []
    for cc in range(ncb):
        c = cc + d * (ncb - 1 - 2 * cc)
        r0 = pl.multiple_of(c * CHUNK, CHUNK)
        rows.append(r0)
        bx = e_ref[pl.ds(r0, CHUNK), 0:HW]
        gx = e_ref[pl.ds(r0, CHUNK), HW:2 * HW]
        gend = e_ref[pl.ds(r0 + tend, 1), HW:2 * HW]
        eg = jnp.exp(gx)
        kdf = jnp.exp(gend - gx)
        ges.append(jnp.exp(gend))
        q16 = q_ref[pl.ds(r0, CHUNK), :]
        k16 = k_ref[pl.ds(r0, CHUNK), :]
        q = q16.astype(F32)
        k = k16.astype(F32)
        v = v_ref[pl.ds(r0, CHUNK), :].astype(F32)
        kbeta = k * bx
        vb = (v * bx).astype(BF16)
        kbe = (kbeta * eg).astype(BF16)
        qd = (q * eg).astype(BF16)
        kd = (k * kdf).astype(BF16)
        for h in heads:
            sl = slice(h * GDN_DV, (h + 1) * GDN_DV)
            qb.append(q16[:, sl])
            kb.append(k16[:, sl])
            rhs.append(jnp.concatenate([vb[:, sl], kbe[:, sl]], axis=1))
            qde.append(qd[:, sl])
            kde.append(kd[:, sl])
            bcol.append(bx[:, h * GDN_DV:h * GDN_DV + CHUNK])
            grow = ftc_ref[c, pl.ds(F_G + d * GDN_HEADS + h, 1), :]
            gd = gx[:, h * GDN_DV:h * GDN_DV + CHUNK] - grow
            dec.append(jnp.exp(jnp.where(incl, gd, NEG)))
    qkk = [_mm_nt(jnp.concatenate([qi, ki], axis=0), ki) for qi, ki in zip(qb, kb)]
    qk = [t[0:CHUNK] for t in qkk]
    a = [jnp.where(strict, bi * t[CHUNK:2 * CHUNK] * di, 0.0) for bi, t, di in zip(bcol, qkk, dec)]
    tinv = _tri_inverse_all(a, eye, b8, m16, m32, m64)
    uw = [_mm(ti, ri) for ti, ri in zip(tinv, rhs)]
    qkd = [(qi * di).astype(BF16) for qi, di in zip(qk, dec)]

    state = [s_ref[:, h * GDN_DV:(h + 1) * GDN_DV] for h in heads]
    for cc in range(ncb):
        p0 = cc * GDN_HEADS
        sb = [s.astype(BF16) for s in state]
        wq = [_mm(jnp.concatenate([uw[p0 + h][:, GDN_DV:2 * GDN_DV].astype(BF16), qde[p0 + h]], axis=0), sb[h])
              for h in heads]
        v_new = [(uw[p0 + h][:, 0:GDN_DV] - wq[h][0:CHUNK]).astype(BF16) for h in heads]
        outs = [wq[h][CHUNK:2 * CHUNK] + _mm(qkd[p0 + h], v_new[h]) for h in heads]
        state = [state[h] * ges[cc][:, h * GDN_DV:(h + 1) * GDN_DV] + _mm_tn(kde[p0 + h], v_new[h])
                 for h in heads]
        o_ref[pl.ds(rows[cc], CHUNK), :] = jnp.concatenate(outs, axis=1)
    for h in heads:
        s_ref[:, h * GDN_DV:(h + 1) * GDN_DV] = state[h]


def _gdn(act, feat, ftc, ns, seq):
    T = act.shape[0]
    tb = TB_MIX
    nb = seq // tb
    rm = _mix_row_map(nb)
    kern = functools.partial(_gdn_kernel, tb=tb)
    qkv = [pl.BlockSpec((tb, HW), lambda s, d, i, j=j: (rm(s, d, i), j)) for j in range(3)]
    return pl.pallas_call(
        kern,
        grid=(ns, N_DIR, nb),
        in_specs=qkv + [
            pl.BlockSpec((tb, LANE), lambda s, d, i: (rm(s, d, i), 0)),
            pl.BlockSpec((tb // CHUNK, LANE, CHUNK), lambda s, d, i: (rm(s, d, i), 0, 0)),
        ],
        out_specs=pl.BlockSpec((None, tb, HW), lambda s, d, i: (d, rm(s, d, i), 0)),
        out_shape=jax.ShapeDtypeStruct((N_DIR, T, HW), F32),
        scratch_shapes=[pltpu.VMEM((GDN_DK, HW), F32), pltpu.VMEM((tb, 2 * HW), F32)],
        compiler_params=pltpu.CompilerParams(
            dimension_semantics=("parallel", "parallel", "arbitrary"), vmem_limit_bytes=VMEM_LIMIT),
        name="gdn_scan",
    )(act, act, act, feat, ftc)


def _ssd_kernel(x_ref, b_ref, c_ref, f_ref, ftc_ref, dp_ref, o_ref, h_ref, e_ref, *, tb):
    d = pl.program_id(1)
    i = pl.program_id(2)

    @pl.when(i == 0)
    def _():
        h_ref[...] = jnp.zeros(h_ref.shape, F32)

    ncb = tb // CHUNK
    _expand_features(f_ref, e_ref, [(HW, F_DT, SSD_HEADS, SSD_HEADDIM), (HW, F_ACS, SSD_HEADS, SSD_HEADDIM),
                                    (SSD_HEADS * LANE, F_ACS, SSD_HEADS, LANE)], d)
    _, _, incl, _ = _dir_masks(d)
    tend = (CHUNK - 1) * (1 - d)
    lane = lax.broadcasted_iota(jnp.int32, (CHUNK, LANE), 1)
    lo = lane < SSD_HEADDIM
    gw = HW // SSD_GROUPS
    epg = SSD_HEADS // SSD_GROUPS
    dfac = (1 - d).astype(F32)

    groups = range(SSD_GROUPS)
    rows, xs, eas, cds, cgs, bgs, xdtes, xhs, lms = [], [], [], [], [], [], [], [], []
    for cc in range(ncb):
        c = cc + d * (ncb - 1 - 2 * cc)
        r0 = pl.multiple_of(c * CHUNK, CHUNK)
        rows.append(r0)
        dtx = e_ref[pl.ds(r0, CHUNK), 0:HW]
        ax = e_ref[pl.ds(r0, CHUNK), HW:2 * HW]
        aend = e_ref[pl.ds(r0 + tend, 1), HW:2 * HW]
        x = x_ref[pl.ds(r0, CHUNK), :].astype(F32)
        xdt = x * dtx
        xdte = (xdt * jnp.exp(aend - ax)).astype(BF16)
        xs.append(x)
        eas.append(jnp.exp(ax))
        cds.append(jnp.exp(aend))
        bm = b_ref[pl.ds(r0, CHUNK), :]
        cm = c_ref[pl.ds(r0, CHUNK), :]
        for g in groups:
            bgs.append(bm[:, g * SSD_STATE:(g + 1) * SSD_STATE])
            cgs.append(cm[:, g * SSD_STATE:(g + 1) * SSD_STATE])
            xdtes.append(xdte[:, g * gw:(g + 1) * gw])
        for hh in range(SSD_HEADS):
            xp = xdt[:, (hh // 2) * LANE:(hh // 2 + 1) * LANE]
            xh = jnp.where(lo, xp, 0.0) if hh % 2 == 0 else jnp.where(lo, 0.0, xp)
            xhs.append(xh.astype(BF16))
            arow = ftc_ref[c, pl.ds(F_ACS + d * SSD_HEADS + hh, 1), :]
            acol = e_ref[pl.ds(r0, CHUNK), 2 * HW + hh * LANE:2 * HW + hh * LANE + CHUNK]
            lms.append(jnp.exp(jnp.where(incl, acol - arow, NEG)))
    cbs = [_mm_nt(cg, bg) for cg, bg in zip(cgs, bgs)]
    sts = [_mm_tn(bg, xe) for bg, xe in zip(bgs, xdtes)]
    yds = [_mm(cbs[(j // SSD_HEADS) * SSD_GROUPS + (j % SSD_HEADS) // epg] * lms[j], xhs[j])
           for j in range(ncb * SSD_HEADS)]

    state = [h_ref[:, g * gw:(g + 1) * gw] for g in groups]
    starts = []
    for cc in range(ncb):
        starts.append([s.astype(BF16) for s in state])
        state = [state[g] * cds[cc][:, g * gw:(g + 1) * gw] + sts[cc * SSD_GROUPS + g] for g in groups]
    for g in groups:
        h_ref[:, g * gw:(g + 1) * gw] = state[g]
    for cc in range(ncb):
        outs = []
        for g in groups:
            y_g = _mm(cgs[cc * SSD_GROUPS + g], starts[cc][g]) * eas[cc][:, g * gw:(g + 1) * gw]
            j0 = cc * SSD_HEADS + g * epg
            yd = [yds[j0 + 2 * pr] + yds[j0 + 2 * pr + 1] for pr in range(epg // 2)]
            outs.append(y_g + jnp.concatenate(yd, axis=1))
        y = jnp.concatenate(outs, axis=1) + xs[cc] * (dp_ref[...] * dfac)
        o_ref[pl.ds(rows[cc], CHUNK), :] = y


def _ssd(act, feat, ftc, d_row, ns, seq):
    T = act.shape[0]
    tb = TB_MIX
    nb = seq // tb
    rm = _mix_row_map(nb)
    kern = functools.partial(_ssd_kernel, tb=tb)
    bc_w = SSD_GROUPS * SSD_STATE
    return pl.pallas_call(
        kern,
        grid=(ns, N_DIR, nb),
        in_specs=[
            pl.BlockSpec((tb, HW), lambda s, d, i: (rm(s, d, i), COL_SXBC // HW)),
            pl.BlockSpec((tb, bc_w), lambda s, d, i: (rm(s, d, i), (COL_SXBC + HW) // bc_w)),
            pl.BlockSpec((tb, bc_w), lambda s, d, i: (rm(s, d, i), (COL_SXBC + HW) // bc_w + 1)),
            pl.BlockSpec((tb, LANE), lambda s, d, i: (rm(s, d, i), 0)),
            pl.BlockSpec((tb // CHUNK, LANE, CHUNK), lambda s, d, i: (rm(s, d, i), 0, 0)),
            pl.BlockSpec((1, HW), lambda s, d, i: (0, 0)),
        ],
        out_specs=pl.BlockSpec((None, tb, HW), lambda s, d, i: (d, rm(s, d, i), 0)),
        out_shape=jax.ShapeDtypeStruct((N_DIR, T, HW), F32),
        scratch_shapes=[pltpu.VMEM((SSD_STATE, HW), F32), pltpu.VMEM((tb, 2 * HW + SSD_HEADS * LANE), F32)],
        compiler_params=pltpu.CompilerParams(
            dimension_semantics=("parallel", "parallel", "arbitrary"), vmem_limit_bytes=VMEM_LIMIT),
        name="ssd_scan",
    )(act, act, act, feat, ftc, d_row)


def _mlstm_kernel(q_ref, k_ref, v_ref, f_ref, ftc_ref, o_ref, c_ref, m_ref, e_ref, *, tb):
    d = pl.program_id(1)
    i = pl.program_id(2)

    @pl.when(i == 0)
    def _():
        c_ref[...] = jnp.zeros(c_ref.shape, F32)
        m_ref[...] = jnp.zeros(m_ref.shape, F32)

    ncb = tb // CHUNK
    dh = MLSTM_DH
    _expand_features(f_ref, e_ref, [(HW, F_IPRE, MLSTM_HEADS, dh), (HW, F_BCUM, MLSTM_HEADS, dh)], d)
    _, _, incl, _ = _dir_masks(d)
    tend = (CHUNK - 1) * (1 - d)
    ones = jnp.ones((CHUNK, dh), BF16)

    heads = range(MLSTM_HEADS)
    rows, bxs, bends, mcs = [], [], [], []
    qb, kb, kw, vaug, dmat, mis = [], [], [], [], [], []
    for cc in range(ncb):
        c = cc + d * (ncb - 1 - 2 * cc)
        r0 = pl.multiple_of(c * CHUNK, CHUNK)
        rows.append(r0)
        ix = e_ref[pl.ds(r0, CHUNK), 0:HW]
        bx = e_ref[pl.ds(r0, CHUNK), HW:2 * HW]
        bend = e_ref[pl.ds(r0 + tend, 1), HW:2 * HW]
        logw = bend - bx + ix
        mc = jnp.max(logw, axis=0, keepdims=True)
        ws = jnp.exp(logw - mc)
        bxs.append(bx)
        bends.append(bend)
        mcs.append(mc)
        q = q_ref[pl.ds(r0, CHUNK), :].astype(BF16)
        k = k_ref[pl.ds(r0, CHUNK), :] * (dh ** -0.5)
        kws = (k * ws).astype(BF16)
        k = k.astype(BF16)
        v = v_ref[pl.ds(r0, CHUNK), :].astype(BF16)
        for h in heads:
            sl = slice(h * dh, (h + 1) * dh)
            qb.append(q[:, sl])
            kb.append(k[:, sl])
            kw.append(kws[:, sl])
            vaug.append(jnp.concatenate([v[:, sl], ones], axis=1))
            brow = ftc_ref[c, pl.ds(F_BCUM + d * MLSTM_HEADS + h, 1), :]
            irow = ftc_ref[c, pl.ds(F_IPRE + d * MLSTM_HEADS + h, 1), :]
            logd = jnp.where(incl, bx[:, h * dh:h * dh + CHUNK] - brow + irow, NEG)
            mi = jnp.max(logd, axis=-1, keepdims=True)
            mis.append(mi)
            dmat.append(jnp.exp(logd - mi))
    qk = [_mm_nt(qi, ki) for qi, ki in zip(qb, kb)]
    kv = [_mm_tn(ki, vi) for ki, vi in zip(kw, vaug)]
    intra = [_mm(qi * di, vi) for qi, di, vi in zip(qk, dmat, vaug)]

    state = [c_ref[:, h * 2 * dh:(h + 1) * 2 * dh] for h in heads]
    m_old = m_ref[...]
    starts, m_starts = [], []
    for cc in range(ncb):
        starts.append([s.astype(BF16) for s in state])
        m_starts.append(m_old)
        m_new = jnp.maximum(bends[cc] + m_old, mcs[cc])
        s_old = jnp.exp(bends[cc] + m_old - m_new)
        s_add = jnp.exp(mcs[cc] - m_new)
        new_state = []
        for h in heads:
            sl = slice(h * dh, (h + 1) * dh)
            so = jnp.concatenate([s_old[:, sl], s_old[:, sl]], axis=1)
            sa = jnp.concatenate([s_add[:, sl], s_add[:, sl]], axis=1)
            new_state.append(state[h] * so + kv[cc * MLSTM_HEADS + h] * sa)
        state = new_state
        m_old = m_new
    for h in heads:
        c_ref[:, h * 2 * dh:(h + 1) * 2 * dh] = state[h]
    m_ref[...] = m_old
    inter = [_mm(qb[cc * MLSTM_HEADS + h], starts[cc][h]) for cc in range(ncb) for h in heads]
    for cc in range(ncb):
        m_inter = bxs[cc] + m_starts[cc]
        outs = []
        for h in heads:
            j = cc * MLSTM_HEADS + h
            mint = m_inter[:, h * dh:(h + 1) * dh]
            m_t = jnp.maximum(mint, mis[j])
            s_inter = jnp.exp(mint - m_t)
            s_intra = jnp.exp(mis[j] - m_t)
            num = s_inter * inter[j][:, 0:dh] + s_intra * intra[j][:, 0:dh]
            den = s_inter * inter[j][:, dh:2 * dh] + s_intra * intra[j][:, dh:2 * dh]
            outs.append(num / jnp.maximum(jnp.abs(den), jnp.exp(-m_t)))
        o_ref[pl.ds(rows[cc], CHUNK), :] = jnp.concatenate(outs, axis=1)


def _mlstm(proj, feat, ftc, ns, seq):
    T = proj.shape[0]
    tb = TB_MIX
    nb = seq // tb
    rm = _mix_row_map(nb)
    kern = functools.partial(_mlstm_kernel, tb=tb)
    qkv = [pl.BlockSpec((tb, HW), lambda s, d, i, j=j: (rm(s, d, i), COL_CQKV // HW + j)) for j in range(3)]
    return pl.pallas_call(
        kern,
        grid=(ns, N_DIR, nb),
        in_specs=qkv + [
            pl.BlockSpec((tb, LANE), lambda s, d, i: (rm(s, d, i), 0)),
            pl.BlockSpec((tb // CHUNK, LANE, CHUNK), lambda s, d, i: (rm(s, d, i), 0, 0)),
        ],
        out_specs=pl.BlockSpec((None, tb, HW), lambda s, d, i: (d, rm(s, d, i), 0)),
        out_shape=jax.ShapeDtypeStruct((N_DIR, T, HW), F32),
        scratch_shapes=[pltpu.VMEM((MLSTM_DH, 2 * HW), F32), pltpu.VMEM((1, HW), F32),
                        pltpu.VMEM((tb, 2 * HW), F32)],
        compiler_params=pltpu.CompilerParams(
            dimension_semantics=("parallel", "parallel", "arbitrary"), vmem_limit_bytes=VMEM_LIMIT),
        name="mlstm_scan",
    )(proj, proj, proj, feat, ftc)


def _post_kernel(og_ref, ys_ref, hm_ref, az_ref, sz_ref, co_ref, x_ref, wo_ref, nw_ref, ln_ref, o_ref):
    oa = og_ref[0] + og_ref[1]
    az = az_ref[...]
    parts = []
    for h in range(GDN_HEADS):
        sl = slice(h * GDN_DV, (h + 1) * GDN_DV)
        seg = oa[:, sl]
        ms = jnp.mean(seg * seg, -1, keepdims=True)
        parts.append(seg * lax.rsqrt(ms + NORM_EPS) * nw_ref[0:1, sl] * _silu(az[:, sl]))
    ys = (ys_ref[0] + ys_ref[1]) * _silu(sz_ref[...])
    gw = HW // SSD_GROUPS
    for g in range(SSD_GROUPS):
        sl = slice(g * gw, (g + 1) * gw)
        seg = ys[:, sl]
        ms = jnp.mean(seg * seg, -1, keepdims=True)
        parts.append(seg * lax.rsqrt(ms + NORM_EPS) * nw_ref[1:2, sl])
    hc = hm_ref[0] + hm_ref[1]
    co = co_ref[...]
    for h in range(MLSTM_HEADS):
        sl = slice(h * MLSTM_DH, (h + 1) * MLSTM_DH)
        seg = hc[:, sl]
        seg = seg - jnp.mean(seg, -1, keepdims=True)
        ms = jnp.mean(seg * seg, -1, keepdims=True)
        parts.append(_sigmoid(co[:, sl]) * (seg * lax.rsqrt(ms + NORM_EPS) * nw_ref[2:3, sl]))
    y = jnp.concatenate(parts, axis=1).astype(BF16)
    hmix = jnp.dot(y, wo_ref[...], preferred_element_type=F32)
    z = DN_ALPHA * x_ref[...] + hmix
    o_ref[...] = _layernorm_rows(z, ln_ref[0:1, :], ln_ref[1:2, :])


def _post(og, ys, hm, proj, x, w_out, norm_w, ln):
    T, D = x.shape
    tb = TB_POST
    dir_spec = pl.BlockSpec((N_DIR, tb, HW), lambda i: (0, i, 0))
    zc = COL_Z // HW
    return pl.pallas_call(
        _post_kernel,
        grid=(T // tb,),
        in_specs=[dir_spec, dir_spec, dir_spec,
                  pl.BlockSpec((tb, HW), lambda i: (i, zc)),
                  pl.BlockSpec((tb, HW), lambda i: (i, zc + 1)),
                  pl.BlockSpec((tb, HW), lambda i: (i, zc + 2)),
                  pl.BlockSpec((tb, D), lambda i: (i, 0)),
                  pl.BlockSpec(w_out.shape, lambda i: (0, 0)),
                  pl.BlockSpec((SUBLANE, HW), lambda i: (0, 0)),
                  pl.BlockSpec((SUBLANE, D), lambda i: (0, 0))],
        out_specs=pl.BlockSpec((tb, D), lambda i: (i, 0)),
        out_shape=jax.ShapeDtypeStruct((T, D), F32),
        compiler_params=pltpu.CompilerParams(dimension_semantics=("parallel",),
                                             vmem_limit_bytes=VMEM_LIMIT),
        name="post",
    )(og, ys, hm, proj, proj, proj, x, w_out, norm_w, ln)


def _ffn_kernel(x_ref, wg_ref, wu_ref, wd_ref, ln_ref, o_ref, acc_ref, *, nf):
    f = pl.program_id(1)
    xb = x_ref[...].astype(BF16)
    g = jnp.dot(xb, wg_ref[...], preferred_element_type=F32)
    u = jnp.dot(xb, wu_ref[...], preferred_element_type=F32)
    contrib = jnp.dot((_silu(g) * u).astype(BF16), wd_ref[...], preferred_element_type=F32)

    @pl.when(f == 0)
    def _():
        acc_ref[...] = contrib

    @pl.when(f > 0)
    def _():
        acc_ref[...] += contrib

    @pl.when(f == nf - 1)
    def _():
        z = DN_ALPHA * x_ref[...] + acc_ref[...]
        o_ref[...] = _layernorm_rows(z, ln_ref[0:1, :], ln_ref[1:2, :])


def _ffn(x, wg, wu, wd, ln):
    T, D = x.shape
    F = wg.shape[1]
    tm = TM_FFN
    tf = F // 2
    nf = F // tf
    kern = functools.partial(_ffn_kernel, nf=nf)
    return pl.pallas_call(
        kern,
        grid=(T // tm, nf),
        in_specs=[pl.BlockSpec((tm, D), lambda i, f: (i, 0)),
                  pl.BlockSpec((D, tf), lambda i, f: (0, f)),
                  pl.BlockSpec((D, tf), lambda i, f: (0, f)),
                  pl.BlockSpec((tf, D), lambda i, f: (f, 0)),
                  pl.BlockSpec((SUBLANE, D), lambda i, f: (0, 0))],
        out_specs=pl.BlockSpec((tm, D), lambda i, f: (i, 0)),
        out_shape=jax.ShapeDtypeStruct((T, D), F32),
        scratch_shapes=[pltpu.VMEM((tm, D), F32)],
        compiler_params=pltpu.CompilerParams(dimension_semantics=("parallel", "arbitrary"),
                                             vmem_limit_bytes=VMEM_LIMIT),
        name="ffn",
    )(x, wg, wu, wd, ln)


I_E1, I_E2, I_R1, I_R2, I_G1, I_G2 = range(6)


def _route_kernel(x_ref, r_ref, info_ref, cnt_ref, run_ref, *, ne):
    @pl.when(pl.program_id(0) == 0)
    def _():
        run_ref[...] = jnp.zeros(run_ref.shape, F32)

    logits = jnp.dot(x_ref[...], r_ref[...], precision=HIGHEST, preferred_element_type=F32)
    tm = logits.shape[0]
    lane = lax.broadcasted_iota(jnp.int32, logits.shape, 1)
    lanef = lane.astype(F32)
    logits = jnp.where(lane < ne, logits, NEG)
    m1 = jnp.max(logits, -1, keepdims=True)
    i1 = jnp.min(jnp.where(logits == m1, lanef, float(LANE)), -1, keepdims=True)
    oh1 = lanef == i1
    rest = jnp.where(oh1, NEG, logits)
    m2 = jnp.max(rest, -1, keepdims=True)
    i2 = jnp.min(jnp.where(rest == m2, lanef, float(LANE)), -1, keepdims=True)
    oh2 = lanef == i2
    e2 = jnp.exp(m2 - m1)
    g1 = 1.0 / (1.0 + e2)
    g2 = e2 / (1.0 + e2)
    occ = jnp.where(oh1, 1.0, 0.0) + jnp.where(oh2, 1.0, 0.0)
    r = lax.broadcasted_iota(jnp.int32, (tm, tm), 0)
    c = lax.broadcasted_iota(jnp.int32, (tm, tm), 1)
    earlier = jnp.where(c < r, 1.0, 0.0).astype(BF16)
    base = run_ref[...] + jnp.dot(earlier, occ.astype(BF16), preferred_element_type=F32)
    rank1 = jnp.sum(jnp.where(oh1, base, 0.0), -1, keepdims=True)
    rank2 = jnp.sum(jnp.where(oh2, base, 0.0), -1, keepdims=True)
    total = run_ref[...] + jnp.sum(occ, axis=0, keepdims=True)
    run_ref[...] = total
    cnt_ref[...] = jnp.broadcast_to(total, cnt_ref.shape)
    info = jnp.zeros(logits.shape, F32)
    for idx, val in ((I_E1, i1), (I_E2, i2), (I_R1, rank1), (I_R2, rank2), (I_G1, g1), (I_G2, g2)):
        info = jnp.where(lane == idx, val, info)
    info_ref[...] = info


def _route(x, router):
    T, D = x.shape
    tm = TM_ROUTE
    kern = functools.partial(_route_kernel, ne=N_EXPERTS)
    return pl.pallas_call(
        kern,
        grid=(T // tm,),
        in_specs=[pl.BlockSpec((tm, D), lambda i: (i, 0)),
                  pl.BlockSpec((D, LANE), lambda i: (0, 0))],
        out_specs=[pl.BlockSpec((tm, LANE), lambda i: (i, 0)),
                   pl.BlockSpec((SUBLANE, LANE), lambda i: (0, 0))],
        out_shape=[jax.ShapeDtypeStruct((T, LANE), F32),
                   jax.ShapeDtypeStruct((SUBLANE, LANE), F32)],
        scratch_shapes=[pltpu.VMEM((1, LANE), F32)],
        compiler_params=pltpu.CompilerParams(dimension_semantics=("arbitrary",),
                                             vmem_limit_bytes=VMEM_LIMIT),
        name="route",
    )(x, router)


def _row_copy(src_ref, src_row, dst_ref, dst_row, sem):
    return pltpu.make_async_copy(src_ref.at[pl.ds(src_row, 1), :], dst_ref.at[pl.ds(dst_row, 1), :], sem)


def _dispatch_kernel(p1_ref, p2_ref, x_ref, xs_in_ref, xs_ref, sem, *, tm):
    del xs_in_ref

    def start(r, carry):
        _row_copy(x_ref, r, xs_ref, p1_ref[0, 0, r], sem).start()
        _row_copy(x_ref, r, xs_ref, p2_ref[0, 0, r], sem).start()
        return carry

    def wait(r, carry):
        _row_copy(x_ref, r, xs_ref, p1_ref[0, 0, r], sem).wait()
        _row_copy(x_ref, r, xs_ref, p2_ref[0, 0, r], sem).wait()
        return carry

    lax.fori_loop(0, tm, start, 0)
    lax.fori_loop(0, tm, wait, 0)


def _dispatch(x, pos1, pos2, n_rows):
    T, D = x.shape
    tm = TM_ROUTE
    kern = functools.partial(_dispatch_kernel, tm=tm)
    pos_spec = pl.BlockSpec((1, 1, tm), lambda i: (i, 0, 0), memory_space=pltpu.SMEM)
    return pl.pallas_call(
        kern,
        grid=(T // tm,),
        in_specs=[pos_spec, pos_spec,
                  pl.BlockSpec((tm, D), lambda i: (i, 0)),
                  pl.BlockSpec(memory_space=pl.ANY)],
        out_specs=pl.BlockSpec(memory_space=pl.ANY),
        out_shape=jax.ShapeDtypeStruct((n_rows, D), F32),
        scratch_shapes=[pltpu.SemaphoreType.DMA(())],
        input_output_aliases={3: 0},
        compiler_params=pltpu.CompilerParams(dimension_semantics=("arbitrary",),
                                             vmem_limit_bytes=VMEM_LIMIT),
        name="dispatch",
    )(pos1, pos2, x, jnp.zeros((n_rows, D), F32))


def _gffn_kernel(te_ref, tv_ref, x_ref, wg_ref, wu_ref, wd_ref, y_ref, *, tf):
    del te_ref
    valid = tv_ref[pl.program_id(0)] == 1

    @pl.when(valid)
    def _():
        xb = x_ref[...].astype(BF16)
        acc = jnp.zeros(y_ref.shape, F32)
        for f0 in range(0, wg_ref.shape[1], tf):
            g = jnp.dot(xb, wg_ref[:, f0:f0 + tf], preferred_element_type=F32)
            u = jnp.dot(xb, wu_ref[:, f0:f0 + tf], preferred_element_type=F32)
            acc = acc + jnp.dot((_silu(g) * u).astype(BF16), wd_ref[f0:f0 + tf, :],
                                preferred_element_type=F32)
        y_ref[...] = acc

    @pl.when(jnp.logical_not(valid))
    def _():
        y_ref[...] = jnp.zeros(y_ref.shape, F32)


def _gffn(xs, tile_expert, tile_valid, wg, wu, wd):
    P, D = xs.shape
    _, _, F = wg.shape
    tm = TM_GFFN
    kern = functools.partial(_gffn_kernel, tf=TF_MOE)
    grid_spec = pltpu.PrefetchScalarGridSpec(
        num_scalar_prefetch=2,
        grid=(P // tm,),
        in_specs=[pl.BlockSpec((tm, D), lambda i, te, tv: (i, 0)),
                  pl.BlockSpec((None, D, F), lambda i, te, tv: (te[i], 0, 0), pipeline_mode=pl.Buffered(1)),
                  pl.BlockSpec((None, D, F), lambda i, te, tv: (te[i], 0, 0), pipeline_mode=pl.Buffered(1)),
                  pl.BlockSpec((None, F, D), lambda i, te, tv: (te[i], 0, 0), pipeline_mode=pl.Buffered(1))],
        out_specs=pl.BlockSpec((tm, D), lambda i, te, tv: (i, 0)),
    )
    return pl.pallas_call(
        kern,
        grid_spec=grid_spec,
        out_shape=jax.ShapeDtypeStruct((P, D), F32),
        compiler_params=pltpu.CompilerParams(dimension_semantics=("arbitrary",),
                                             vmem_limit_bytes=VMEM_LIMIT),
        name="gffn",
    )(tile_expert, tile_valid, xs, wg, wu, wd)


def _combine_kernel(p1_ref, p2_ref, x_ref, info_ref, y_ref, ln_ref, o_ref, ya_ref, yb_ref, sem, *, tm):
    def start(r, carry):
        _row_copy(y_ref, p1_ref[0, 0, r], ya_ref, r, sem).start()
        _row_copy(y_ref, p2_ref[0, 0, r], yb_ref, r, sem).start()
        return carry

    def wait(r, carry):
        _row_copy(y_ref, p1_ref[0, 0, r], ya_ref, r, sem).wait()
        _row_copy(y_ref, p2_ref[0, 0, r], yb_ref, r, sem).wait()
        return carry

    lax.fori_loop(0, tm, start, 0)
    lax.fori_loop(0, tm, wait, 0)
    info = info_ref[...]
    g1 = info[:, I_G1:I_G1 + 1]
    g2 = info[:, I_G2:I_G2 + 1]
    z = DN_ALPHA * x_ref[...] + (g1 * ya_ref[...] + g2 * yb_ref[...])
    o_ref[...] = _layernorm_rows(z, ln_ref[0:1, :], ln_ref[1:2, :])


def _combine(x, info, y, pos1, pos2, ln):
    T, D = x.shape
    tm = TM_ROUTE
    kern = functools.partial(_combine_kernel, tm=tm)
    pos_spec = pl.BlockSpec((1, 1, tm), lambda i: (i, 0, 0), memory_space=pltpu.SMEM)
    return pl.pallas_call(
        kern,
        grid=(T // tm,),
        in_specs=[pos_spec, pos_spec,
                  pl.BlockSpec((tm, D), lambda i: (i, 0)),
                  pl.BlockSpec((tm, LANE), lambda i: (i, 0)),
                  pl.BlockSpec(memory_space=pl.ANY),
                  pl.BlockSpec((SUBLANE, D), lambda i: (0, 0))],
        out_specs=pl.BlockSpec((tm, D), lambda i: (i, 0)),
        out_shape=jax.ShapeDtypeStruct((T, D), F32),
        scratch_shapes=[pltpu.VMEM((tm, D), F32), pltpu.VMEM((tm, D), F32), pltpu.SemaphoreType.DMA(())],
        compiler_params=pltpu.CompilerParams(dimension_semantics=("arbitrary",),
                                             vmem_limit_bytes=VMEM_LIMIT),
        name="combine",
    )(pos1, pos2, x, info, y, ln)


def _moe(x, router, wg, wu, wd, ln):
    T, D = x.shape
    ne = wg.shape[0]
    tm = TM_GFFN
    n_tiles = (N_TOP * T) // tm + ne
    info, cnt = _route(x, router)
    counts = cnt[0, :ne].astype(jnp.int32)
    tiles_e = (counts + tm - 1) // tm
    tile_end = jnp.cumsum(tiles_e)
    offs = (tile_end - tiles_e) * tm
    tid = jnp.arange(n_tiles, dtype=jnp.int32)
    tile_expert = jnp.minimum(jnp.sum(tid[:, None] >= tile_end[None, :], axis=1), ne - 1).astype(jnp.int32)
    tile_valid = (tid < tile_end[-1]).astype(jnp.int32)
    eids = jnp.arange(ne, dtype=jnp.int32)[None, :]

    def slot(e_lane, r_lane):
        e = info[:, e_lane].astype(jnp.int32)
        off = jnp.sum(jnp.where(e[:, None] == eids, offs[None, :], 0), axis=1)
        return (off + info[:, r_lane].astype(jnp.int32)).reshape(T // TM_ROUTE, 1, TM_ROUTE)

    pos1 = slot(I_E1, I_R1)
    pos2 = slot(I_E2, I_R2)
    xs = _dispatch(x, pos1, pos2, n_tiles * tm)
    y = _gffn(xs, tile_expert, tile_valid, wg, wu, wd)
    return _combine(x, info, y, pos1, pos2, ln)


def _pad_rows(a, rows):
    return jnp.concatenate([a, jnp.zeros((rows - a.shape[0],) + a.shape[1:], a.dtype)], axis=0)


def _pack_w_in(w):
    gq = GDN_HEADS * GDN_DK
    gv = GDN_HEADS * GDN_DV
    sw = SSD_HEADS * SSD_HEADDIM
    sbc = SSD_GROUPS * SSD_STATE
    mw = MLSTM_HEADS * MLSTM_DH
    sizes = (gq, gq, gv, gv, N_DIR * GDN_HEADS, N_DIR * GDN_HEADS,
             sw, sw, sbc, sbc, N_DIR * SSD_HEADS,
             mw, mw, mw, mw, N_DIR * MLSTM_HEADS, N_DIR * MLSTM_HEADS)
    idx = np.cumsum(sizes)[:-1].tolist()
    (a_q, a_k, a_v, a_z, a_b, a_a, s_z, s_x, s_b, s_c, s_dt,
     c_q, c_k, c_v, c_o, c_i, c_f) = jnp.split(w, idx, axis=1)
    gates = jnp.concatenate([a_b, a_a, s_dt, s_dt, c_i, c_f], axis=1)
    gates = jnp.concatenate([gates, jnp.zeros((w.shape[0], LANE - gates.shape[1]), w.dtype)], axis=1)
    packed = jnp.concatenate([a_q, a_k, a_v, s_x, s_b, s_c, c_q, c_k, c_v, a_z, s_z, c_o, gates], axis=1)
    return packed.astype(BF16)


def _pack_gate_params(a_A_log, a_dt_bias, b_A_log, b_dt_bias, c_i_bias, c_f_bias):
    z8 = jnp.zeros((N_DIR * GDN_HEADS,), F32)
    z16 = jnp.zeros((N_DIR * SSD_HEADS,), F32)
    bias = jnp.concatenate([z8, a_dt_bias.reshape(-1), b_dt_bias.reshape(-1), b_dt_bias.reshape(-1),
                            c_i_bias.reshape(-1), c_f_bias.reshape(-1)])
    coef = jnp.concatenate([z8, -jnp.exp(a_A_log.reshape(-1)), z16, -jnp.exp(b_A_log.reshape(-1)), z8, z8])
    rows = jnp.stack([bias, coef]).astype(F32)
    rows = jnp.concatenate([rows, jnp.zeros((2, LANE - rows.shape[1]), F32)], axis=1)
    return _pad_rows(rows, SUBLANE)


def _mixer_layer(x, ns, seq, w_in, conv_a_w, a_A_log, a_dt_bias, a_norm_w, conv_b_w, conv_b_b,
                 b_A_log, b_dt_bias, b_D, b_norm_w, c_i_bias, c_f_bias, c_norm_w, w_out, ln_g, ln_b):
    proj = _in_proj(x, _pack_w_in(w_in))
    conv_w = _pad_rows(jnp.concatenate([conv_a_w, conv_b_w], axis=1).astype(F32), SUBLANE)
    conv_b = jnp.concatenate([jnp.zeros((conv_a_w.shape[1],), F32), conv_b_b.astype(F32)])[None, :]
    gate_p = _pack_gate_params(a_A_log, a_dt_bias, b_A_log, b_dt_bias, c_i_bias, c_f_bias)
    act, feat, ftc = _prep(proj, conv_w, conv_b, gate_p, ns, seq)
    og = _gdn(act, feat, ftc, ns, seq)
    d_row = jnp.repeat(b_D.astype(F32), SSD_HEADDIM)[None, :]
    ys = _ssd(act, feat, ftc, d_row, ns, seq)
    hm = _mlstm(proj, feat, ftc, ns, seq)
    norm_w = _pad_rows(jnp.stack([jnp.tile(a_norm_w, GDN_HEADS), b_norm_w, c_norm_w]).astype(F32), SUBLANE)
    ln = _pad_rows(jnp.stack([ln_g, ln_b]).astype(F32), SUBLANE)
    return _post(og, ys, hm, proj, x, w_out.astype(BF16), norm_w, ln)


def kernel(x_prompt, x_sample, w_in, conv_a_w, a_A_log, a_dt_bias, a_norm_w, conv_b_w, conv_b_b, b_A_log, b_dt_bias, b_D, b_norm_w, c_i_bias, c_f_bias, c_norm_w, w_out, ln1_g, ln1_b, ln2_g, ln2_b, ffn_w_gate, ffn_w_up, ffn_w_down, moe_router, moe_w_gate, moe_w_up, moe_w_down):
    bp, seq, dm = x_prompt.shape
    bs, seq_s, _ = x_sample.shape
    assert seq == seq_s and dm == D_MODEL and w_in.shape[0] == DEPTH
    assert seq % TB_MIX == 0 and seq % TB_PREP == 0
    ns = bp + bs
    x = jnp.concatenate([x_prompt.reshape(bp * seq, dm), x_sample.reshape(bs * seq, dm)], axis=0)
    for l in range(DEPTH):
        x = _mixer_layer(x, ns, seq, w_in[l], conv_a_w[l], a_A_log[l], a_dt_bias[l], a_norm_w[l],
                         conv_b_w[l], conv_b_b[l], b_A_log[l], b_dt_bias[l], b_D[l], b_norm_w[l],
                         c_i_bias[l], c_f_bias[l], c_norm_w[l], w_out[l], ln1_g[l], ln1_b[l])
        ln2 = _pad_rows(jnp.stack([ln2_g[l], ln2_b[l]]).astype(F32), SUBLANE)
        j = l // 2
        if l % 2 == 0:
            x = _ffn(x, ffn_w_gate[j].astype(BF16), ffn_w_up[j].astype(BF16), ffn_w_down[j].astype(BF16), ln2)
        else:
            router = jnp.concatenate(
                [moe_router[j], jnp.zeros((dm, LANE - N_EXPERTS), moe_router.dtype)], axis=1).astype(F32)
            x = _moe(x, router, moe_w_gate[j].astype(BF16), moe_w_up[j].astype(BF16),
                     moe_w_down[j].astype(BF16), ln2)
    y_prompt = x[:bp * seq].reshape(bp, seq, dm)
    y_sample = x[bp * seq:].reshape(bs, seq, dm)
    return (y_prompt, y_sample)
```

```python
import functools

import jax
import jax.numpy as jnp
import numpy as np
from jax import lax
from jax.experimental import pallas as pl
from jax.experimental.pallas import tpu as pltpu

F32 = jnp.float32
BF16 = jnp.bfloat16
HIGHEST = lax.Precision.HIGHEST

D_MODEL = 1024
DEPTH = 2
GDN_HEADS, GDN_DK, GDN_DV = 4, 128, 128
SSD_HEADS, SSD_HEADDIM, SSD_GROUPS, SSD_STATE = 8, 64, 2, 128
MLSTM_HEADS, MLSTM_DH = 4, 128
N_DIR = 2
CONV_W = 4
CONV_PAD_L = (CONV_W - 1) // 2
CHUNK = 64
N_EXPERTS = 8
DN_ALPHA = (2 * DEPTH) ** 0.25
LN_EPS = 1e-5
NORM_EPS = 1e-6
NEG = -1e30

LANE = 128
SUBLANE = 8

HW = 512
COL_AQKV = 0
COL_SXBC = 3 * HW
N_CONV = 5 * HW
COL_GATE = N_CONV
N_PROJ_F32 = COL_GATE + LANE
COL_CQKV = 0
COL_Z = 3 * HW
N_PROJ_BF16 = 6 * HW

F_BETA, F_G, F_DT, F_ACS, F_IPRE, F_BCUM = 0, 8, 16, 32, 48, 56

TM_PROJ = 256
TB_PREP = 256
TB_MIX = 512
TB_POST = 256
TM_FFN = 512
TF_FFN = 256
TM_ROUTE = 512
TM_GFFN = 256
TF_MOE = 512
N_TOP = 2
ROW_UNROLL = 8
GDN_GROUPS = 2
VMEM_LIMIT = 56 * 1024 * 1024


def _sigmoid(x):
    return 1.0 / (1.0 + jnp.exp(-x))


def _silu(x):
    return x * _sigmoid(x)


def _softplus(x):
    return jnp.maximum(x, 0.0) + jnp.log(1.0 + jnp.exp(-jnp.abs(x)))


def _mm(a, b):
    return jnp.dot(a.astype(BF16), b.astype(BF16), preferred_element_type=F32)


def _mm_nt(a, b):
    return lax.dot_general(a.astype(BF16), b.astype(BF16), (((1,), (1,)), ((), ())),
                           preferred_element_type=F32)


def _mm_tn(a, b):
    return lax.dot_general(a.astype(BF16), b.astype(BF16), (((0,), (0,)), ((), ())),
                           preferred_element_type=F32)


def _layernorm_rows(z, g, b):
    mu = jnp.mean(z, -1, keepdims=True)
    zc = z - mu
    var = jnp.mean(zc * zc, -1, keepdims=True)
    return zc * lax.rsqrt(var + LN_EPS) * g + b


def _inproj_kernel(x_ref, w_ref, of_ref, oh_ref):
    xb = x_ref[...].astype(BF16)
    of_ref[...] = jnp.dot(xb, w_ref[:, 0:N_PROJ_F32], preferred_element_type=F32)
    oh_ref[...] = jnp.dot(xb, w_ref[:, N_PROJ_F32:N_PROJ_F32 + N_PROJ_BF16],
                          preferred_element_type=F32).astype(BF16)


def _in_proj(x, w):
    T, D = x.shape
    N = w.shape[1]
    assert N == N_PROJ_F32 + N_PROJ_BF16
    return pl.pallas_call(
        _inproj_kernel,
        grid=(T // TM_PROJ,),
        in_specs=[pl.BlockSpec((TM_PROJ, D), lambda i: (i, 0)),
                  pl.BlockSpec((D, N), lambda i: (0, 0))],
        out_specs=[pl.BlockSpec((TM_PROJ, N_PROJ_F32), lambda i: (i, 0)),
                   pl.BlockSpec((TM_PROJ, N_PROJ_BF16), lambda i: (i, 0))],
        out_shape=[jax.ShapeDtypeStruct((T, N_PROJ_F32), F32),
                   jax.ShapeDtypeStruct((T, N_PROJ_BF16), BF16)],
        compiler_params=pltpu.CompilerParams(dimension_semantics=("parallel",),
                                             vmem_limit_bytes=VMEM_LIMIT),
        name="in_proj",
    )(x, w)


def _prep_kernel(xm_ref, xp_ref, xn_ref, g_ref, cw_ref, cb_ref, gp_ref,
                 act_ref, f_ref, ftc_ref, pad_ref, *, tb, nb):
    b = pl.program_id(1)
    has_prev = (b > 0).astype(F32)
    has_next = (b < nb - 1).astype(F32)
    pad_ref[0:SUBLANE, :] = xp_ref[...] * has_prev
    pad_ref[SUBLANE:SUBLANE + tb, :] = xm_ref[...]
    pad_ref[SUBLANE + tb:2 * SUBLANE + tb, :] = xn_ref[...] * has_next

    base = SUBLANE - CONV_PAD_L
    for c0 in range(0, N_CONV, HW):
        y = cb_ref[:, c0:c0 + HW]
        for j in range(CONV_W):
            y = y + pad_ref[base + j:base + j + tb, c0:c0 + HW] * cw_ref[j:j + 1, c0:c0 + HW]
        y = _silu(y)
        if c0 in (COL_AQKV, COL_AQKV + HW):
            scale = GDN_DK ** -0.5 if c0 == COL_AQKV else 1.0
            parts = []
            for h in range(GDN_HEADS):
                seg = y[:, h * GDN_DK:(h + 1) * GDN_DK]
                ss = jnp.sum(seg * seg, -1, keepdims=True)
                parts.append(seg * lax.rsqrt(ss + NORM_EPS) * scale)
            y = jnp.concatenate(parts, axis=1)
        act_ref[:, c0:c0 + HW] = y.astype(BF16)

    p = g_ref[...]
    z = p + gp_ref[0:1, :]
    coef = gp_ref[1:2, :]
    lane = lax.broadcasted_iota(jnp.int32, p.shape, 1)
    sp = _softplus(z)
    val = jnp.where(lane < F_G, _sigmoid(p),
          jnp.where(lane < F_DT, coef * sp,
          jnp.where(lane < F_ACS, sp,
          jnp.where(lane < F_IPRE, coef * sp,
          jnp.where(lane < F_BCUM, z, -_softplus(-z))))))
    is_cum = ((lane >= F_G) & (lane < F_DT)) | ((lane >= F_ACS) & (lane < F_IPRE)) | (
        (lane >= F_BCUM) & (lane < F_BCUM + 8))
    wide = (lane >= F_DT) & (lane < F_IPRE)
    col_bwd = jnp.where(wide, lane % 16, 2 * (lane % 8)) >= 8
    r = lax.broadcasted_iota(jnp.int32, (tb, tb), 0)
    c = lax.broadcasted_iota(jnp.int32, (tb, tb), 1)
    same = (r // CHUNK) == (c // CHUNK)
    tril = jnp.where(same & (c <= r), 1.0, 0.0).astype(F32)
    triu = jnp.where(same & (c >= r), 1.0, 0.0).astype(F32)
    cs_f = jnp.dot(tril, val, precision=HIGHEST, preferred_element_type=F32)
    cs_b = jnp.dot(triu, val, precision=HIGHEST, preferred_element_type=F32)
    feat = jnp.where(is_cum, jnp.where(col_bwd, cs_b, cs_f), val)
    f_ref[...] = feat
    for k in range(tb // CHUNK):
        ftc_ref[k] = feat[k * CHUNK:(k + 1) * CHUNK, :].T


def _prep(proj, conv_w, conv_b, gate_p, ns, seq):
    T = proj.shape[0]
    tb = TB_PREP
    nb = seq // tb
    t8 = tb // SUBLANE
    n8 = T // SUBLANE
    kern = functools.partial(_prep_kernel, tb=tb, nb=nb)
    return pl.pallas_call(
        kern,
        grid=(ns, nb),
        in_specs=[
            pl.BlockSpec((tb, N_CONV), lambda s, b: (s * nb + b, 0)),
            pl.BlockSpec((SUBLANE, N_CONV), lambda s, b: (jnp.maximum((s * nb + b) * t8 - 1, 0), 0)),
            pl.BlockSpec((SUBLANE, N_CONV), lambda s, b: (jnp.minimum((s * nb + b + 1) * t8, n8 - 1), 0)),
            pl.BlockSpec((tb, LANE), lambda s, b: (s * nb + b, COL_GATE // LANE)),
            pl.BlockSpec((SUBLANE, N_CONV), lambda s, b: (0, 0)),
            pl.BlockSpec((1, N_CONV), lambda s, b: (0, 0)),
            pl.BlockSpec((SUBLANE, LANE), lambda s, b: (0, 0)),
        ],
        out_specs=[
            pl.BlockSpec((tb, N_CONV), lambda s, b: (s * nb + b, 0)),
            pl.BlockSpec((tb, LANE), lambda s, b: (s * nb + b, 0)),
            pl.BlockSpec((tb // CHUNK, LANE, CHUNK), lambda s, b: (s * nb + b, 0, 0)),
        ],
        out_shape=[
            jax.ShapeDtypeStruct((T, N_CONV), BF16),
            jax.ShapeDtypeStruct((T, LANE), F32),
            jax.ShapeDtypeStruct((T // CHUNK, LANE, CHUNK), F32),
        ],
        scratch_shapes=[pltpu.VMEM((tb + 2 * SUBLANE, N_CONV), F32)],
        compiler_params=pltpu.CompilerParams(dimension_semantics=("parallel", "parallel"),
                                             vmem_limit_bytes=VMEM_LIMIT),
        name="prep",
    )(proj, proj, proj, proj, conv_w, conv_b, gate_p)


def _expand_features(f_ref, e_ref, targets, d):
    n = e_ref.shape[1]
    crow = lax.broadcasted_iota(jnp.int32, (LANE, n), 0)
    lcol = lax.broadcasted_iota(jnp.int32, (LANE, n), 1)
    tgt = jnp.zeros((LANE, n), jnp.int32)
    off = 0
    for width, base, heads, per in targets:
        seg = base + d * heads + (lcol - off) // per
        tgt = jnp.where((lcol >= off) & (lcol < off + width), seg, tgt)
        off += width
    sel = jnp.where(crow == tgt, 1.0, 0.0).astype(BF16)
    f = f_ref[...]
    hi = f.astype(BF16)
    r1 = f - hi.astype(F32)
    mid = r1.astype(BF16)
    lo = (r1 - mid.astype(F32)).astype(BF16)
    e_ref[...] = (jnp.dot(jnp.concatenate([hi, mid], axis=1), jnp.concatenate([sel, sel], axis=0),
                          preferred_element_type=F32)
                  + jnp.dot(lo, sel, preferred_element_type=F32))


def _dir_masks(d):
    row = lax.broadcasted_iota(jnp.int32, (CHUNK, CHUNK), 0)
    col = lax.broadcasted_iota(jnp.int32, (CHUNK, CHUNK), 1)
    diff = (row - col) * (1 - 2 * d)
    return row, col, diff >= 0, diff > 0


def _mix_row_map(nb):
    return lambda s, d, i: s * nb + i + d * (nb - 1 - 2 * i)


def _tri_inverse_all(a_list, eye, m8, m16, m32, m64):
    a8 = [jnp.where(m8, a, 0.0) for a in a_list]
    x = [eye - a for a in a8]
    a8 = [a.astype(BF16) for a in a8]
    p2 = [_mm(a, a).astype(BF16) for a in a8]
    yield
    x = [xi + _mm(xi, pi) for xi, pi in zip(x, p2)]
    p4 = [_mm(pi, pi) for pi in p2]
    yield
    x = [xi + _mm(xi, pi) for xi, pi in zip(x, p4)]
    yield
    for m in (m16, m32, m64):
        xb = [t.astype(BF16) for t in x]
        y = [_mm(jnp.where(m, a, 0.0), xi) for a, xi in zip(a_list, xb)]
        yield
        x = [xf - _mm(xi, yi) for xf, xi, yi in zip(x, xb, y)]
        yield
    return x


def _drain(*gens):
    live = list(gens)
    while live:
        for g in list(live):
            try:
                next(g)
            except StopIteration:
                live.remove(g)


def _gdn_kernel(q_ref, k_ref, v_ref, f_ref, ftc_ref, o_ref, s_ref, e_ref, *, tb):
    d = pl.program_id(1)
    i = pl.program_id(2)

    @pl.when(i == 0)
    def _():
        s_ref[...] = jnp.zeros(s_ref.shape, F32)

    ncb = tb // CHUNK
    _expand_features(f_ref, e_ref, [(HW, F_BETA, GDN_HEADS, GDN_DV), (HW, F_G, GDN_HEADS, GDN_DV)], d)
    row, col, incl, strict = _dir_masks(d)
    eye = jnp.where(row == col, 1.0, 0.0).astype(F32)
    b8 = (row // 8) == (col // 8)
    b16 = (row // 16) == (col // 16)
    b32 = (row // 32) == (col // 32)
    m16 = b16 & jnp.logical_not(b8)
    m32 = b32 & jnp.logical_not(b16)
    m64 = jnp.logical_not(b32)
    tend = (CHUNK - 1) * (1 - d)

    heads = range(GDN_HEADS)

    def prepare(chunks, out):
        rows, ges = [], []
        qb, kb, rhs, qde, kde, bcol, dec = [], [], [], [], [], [], []
        for cc in chunks:
            c = cc + d * (ncb - 1 - 2 * cc)
            r0 = pl.multiple_of(c * CHUNK, CHUNK)
            rows.append(r0)
            bx = e_ref[pl.ds(r0, CHUNK), 0:HW]
            gx = e_ref[pl.ds(r0, CHUNK), HW:2 * HW]
            gend = e_ref[pl.ds(r0 + tend, 1), HW:2 * HW]
            eg = jnp.exp(gx)
            kdf = jnp.exp(gend - gx)
            ges.append(jnp.exp(gend))
            q16 = q_ref[pl.ds(r0, CHUNK), :]
            k16 = k_ref[pl.ds(r0, CHUNK), :]
            q = q16.astype(F32)
            k = k16.astype(F32)
            v = v_ref[pl.ds(r0, CHUNK), :].astype(F32)
            kbeta = k * bx
            vb = (v * bx).astype(BF16)
            kbe = (kbeta * eg).astype(BF16)
            qd = (q * eg).astype(BF16)
            kd = (k * kdf).astype(BF16)
            for h in heads:
                sl = slice(h * GDN_DV, (h + 1) * GDN_DV)
                qb.append(q16[:, sl])
                kb.append(k16[:, sl])
                rhs.append(jnp.concatenate([vb[:, sl], kbe[:, sl]], axis=1))
                qde.append(qd[:, sl])
                kde.append(kd[:, sl])
                bcol.append(bx[:, h * GDN_DV:h * GDN_DV + CHUNK])
                grow = ftc_ref[c, pl.ds(F_G + d * GDN_HEADS + h, 1), :]
                gd = gx[:, h * GDN_DV:h * GDN_DV + CHUNK] - grow
                dec.append(jnp.exp(jnp.where(incl, gd, NEG)))
        qkk = [_mm_nt(jnp.concatenate([qi, ki], axis=0), ki) for qi, ki in zip(qb, kb)]
        yield
        a = [jnp.where(strict, bi * t[CHUNK:2 * CHUNK] * di, 0.0) for bi, t, di in zip(bcol, qkk, dec)]
        tinv = yield from _tri_inverse_all(a, eye, b8, m16, m32, m64)
        uw = [_mm(ti, ri) for ti, ri in zip(tinv, rhs)]
        yield
        qkd = [(t[0:CHUNK] * di).astype(BF16) for t, di in zip(qkk, dec)]
        out.update(rows=rows, ges=ges, qde=qde, kde=kde, uw=uw, qkd=qkd)

    state = [s_ref[:, h * GDN_DV:(h + 1) * GDN_DV] for h in heads]

    def recur(pre):
        for n, r0 in enumerate(pre["rows"]):
            p0 = n * GDN_HEADS
            uw, qde, kde, qkd = pre["uw"], pre["qde"], pre["kde"], pre["qkd"]
            sb = [s.astype(BF16) for s in state]
            wq = [_mm(jnp.concatenate([uw[p0 + h][:, GDN_DV:2 * GDN_DV].astype(BF16), qde[p0 + h]], axis=0),
                      sb[h]) for h in heads]
            yield
            v_new = [(uw[p0 + h][:, 0:GDN_DV] - wq[h][0:CHUNK]).astype(BF16) for h in heads]
            outs = [wq[h][CHUNK:2 * CHUNK] + _mm(qkd[p0 + h], v_new[h]) for h in heads]
            for h in heads:
                state[h] = (state[h] * pre["ges"][n][:, h * GDN_DV:(h + 1) * GDN_DV]
                            + _mm_tn(kde[p0 + h], v_new[h]))
            yield
            o_ref[pl.ds(r0, CHUNK), :] = jnp.concatenate(outs, axis=1).astype(BF16)

    gsz = max(ncb // GDN_GROUPS, 1)
    pres = [{} for _ in range(0, ncb, gsz)]
    _drain(prepare(range(0, gsz), pres[0]))
    for n in range(1, len(pres)):
        _drain(prepare(range(n * gsz, (n + 1) * gsz), pres[n]), recur(pres[n - 1]))
    _drain(recur(pres[-1]))
    for h in heads:
        s_ref[:, h * GDN_DV:(h + 1) * GDN_DV] = state[h]


def _gdn(act, feat, ftc, ns, seq):
    T = act.shape[0]
    tb = TB_MIX
    nb = seq // tb
    rm = _mix_row_map(nb)
    kern = functools.partial(_gdn_kernel, tb=tb)
    qkv = [pl.BlockSpec((tb, HW), lambda s, d, i, j=j: (rm(s, d, i), j)) for j in range(3)]
    return pl.pallas_call(
        kern,
        grid=(ns, N_DIR, nb),
        in_specs=qkv + [
            pl.BlockSpec((tb, LANE), lambda s, d, i: (rm(s, d, i), 0)),
            pl.BlockSpec((tb // CHUNK, LANE, CHUNK), lambda s, d, i: (rm(s, d, i), 0, 0)),
        ],
        out_specs=pl.BlockSpec((None, tb, HW), lambda s, d, i: (d, rm(s, d, i), 0)),
        out_shape=jax.ShapeDtypeStruct((N_DIR, T, HW), BF16),
        scratch_shapes=[pltpu.VMEM((GDN_DK, HW), F32), pltpu.VMEM((tb, 2 * HW), F32)],
        compiler_params=pltpu.CompilerParams(
            dimension_semantics=("parallel", "parallel", "arbitrary"), vmem_limit_bytes=VMEM_LIMIT),
        name="gdn_scan",
    )(act, act, act, feat, ftc)


def _ssd_kernel(x_ref, b_ref, c_ref, f_ref, ftc_ref, dp_ref, o_ref, h_ref, e_ref, *, tb):
    d = pl.program_id(1)
    i = pl.program_id(2)

    @pl.when(i == 0)
    def _():
        h_ref[...] = jnp.zeros(h_ref.shape, F32)

    ncb = tb // CHUNK
    _expand_features(f_ref, e_ref, [(HW, F_DT, SSD_HEADS, SSD_HEADDIM), (HW, F_ACS, SSD_HEADS, SSD_HEADDIM),
                                    (SSD_HEADS * LANE, F_ACS, SSD_HEADS, LANE)], d)
    _, _, incl, _ = _dir_masks(d)
    tend = (CHUNK - 1) * (1 - d)
    lane = lax.broadcasted_iota(jnp.int32, (CHUNK, LANE), 1)
    lo = lane < SSD_HEADDIM
    gw = HW // SSD_GROUPS
    epg = SSD_HEADS // SSD_GROUPS
    dfac = (1 - d).astype(F32)

    groups = range(SSD_GROUPS)
    rows, xs, eas, cds, cgs, bgs, xdtes, xhs, lms = [], [], [], [], [], [], [], [], []
    for cc in range(ncb):
        c = cc + d * (ncb - 1 - 2 * cc)
        r0 = pl.multiple_of(c * CHUNK, CHUNK)
        rows.append(r0)
        dtx = e_ref[pl.ds(r0, CHUNK), 0:HW]
        ax = e_ref[pl.ds(r0, CHUNK), HW:2 * HW]
        aend = e_ref[pl.ds(r0 + tend, 1), HW:2 * HW]
        x = x_ref[pl.ds(r0, CHUNK), :].astype(F32)
        xdt = x * dtx
        xdte = (xdt * jnp.exp(aend - ax)).astype(BF16)
        xs.append(x)
        eas.append(jnp.exp(ax))
        cds.append(jnp.exp(aend))
        bm = b_ref[pl.ds(r0, CHUNK), :]
        cm = c_ref[pl.ds(r0, CHUNK), :]
        for g in groups:
            bgs.append(bm[:, g * SSD_STATE:(g + 1) * SSD_STATE])
            cgs.append(cm[:, g * SSD_STATE:(g + 1) * SSD_STATE])
            xdtes.append(xdte[:, g * gw:(g + 1) * gw])
        for hh in range(SSD_HEADS):
            xp = xdt[:, (hh // 2) * LANE:(hh // 2 + 1) * LANE]
            xh = jnp.where(lo, xp, 0.0) if hh % 2 == 0 else jnp.where(lo, 0.0, xp)
            xhs.append(xh.astype(BF16))
            arow = ftc_ref[c, pl.ds(F_ACS + d * SSD_HEADS + hh, 1), :]
            acol = e_ref[pl.ds(r0, CHUNK), 2 * HW + hh * LANE:2 * HW + hh * LANE + CHUNK]
            lms.append(jnp.exp(jnp.where(incl, acol - arow, NEG)))
    cbs = [_mm_nt(cg, bg) for cg, bg in zip(cgs, bgs)]
    sts = [_mm_tn(bg, xe) for bg, xe in zip(bgs, xdtes)]
    yds = [_mm(cbs[(j // SSD_HEADS) * SSD_GROUPS + (j % SSD_HEADS) // epg] * lms[j], xhs[j])
           for j in range(ncb * SSD_HEADS)]

    state = [h_ref[:, g * gw:(g + 1) * gw] for g in groups]
    starts = []
    for cc in range(ncb):
        starts.append([s.astype(BF16) for s in state])
        state = [state[g] * cds[cc][:, g * gw:(g + 1) * gw] + sts[cc * SSD_GROUPS + g] for g in groups]
    for g in groups:
        h_ref[:, g * gw:(g + 1) * gw] = state[g]
    for cc in range(ncb):
        outs = []
        for g in groups:
            y_g = _mm(cgs[cc * SSD_GROUPS + g], starts[cc][g]) * eas[cc][:, g * gw:(g + 1) * gw]
            j0 = cc * SSD_HEADS + g * epg
            yd = [yds[j0 + 2 * pr] + yds[j0 + 2 * pr + 1] for pr in range(epg // 2)]
            outs.append(y_g + jnp.concatenate(yd, axis=1))
        y = jnp.concatenate(outs, axis=1) + xs[cc] * (dp_ref[...] * dfac)
        o_ref[pl.ds(rows[cc], CHUNK), :] = y.astype(BF16)


def _ssd(act, feat, ftc, d_row, ns, seq):
    T = act.shape[0]
    tb = TB_MIX
    nb = seq // tb
    rm = _mix_row_map(nb)
    kern = functools.partial(_ssd_kernel, tb=tb)
    bc_w = SSD_GROUPS * SSD_STATE
    return pl.pallas_call(
        kern,
        grid=(ns, N_DIR, nb),
        in_specs=[
            pl.BlockSpec((tb, HW), lambda s, d, i: (rm(s, d, i), COL_SXBC // HW)),
            pl.BlockSpec((tb, bc_w), lambda s, d, i: (rm(s, d, i), (COL_SXBC + HW) // bc_w)),
            pl.BlockSpec((tb, bc_w), lambda s, d, i: (rm(s, d, i), (COL_SXBC + HW) // bc_w + 1)),
            pl.BlockSpec((tb, LANE), lambda s, d, i: (rm(s, d, i), 0)),
            pl.BlockSpec((tb // CHUNK, LANE, CHUNK), lambda s, d, i: (rm(s, d, i), 0, 0)),
            pl.BlockSpec((1, HW), lambda s, d, i: (0, 0)),
        ],
        out_specs=pl.BlockSpec((None, tb, HW), lambda s, d, i: (d, rm(s, d, i), 0)),
        out_shape=jax.ShapeDtypeStruct((N_DIR, T, HW), BF16),
        scratch_shapes=[pltpu.VMEM((SSD_STATE, HW), F32), pltpu.VMEM((tb, 2 * HW + SSD_HEADS * LANE), F32)],
        compiler_params=pltpu.CompilerParams(
            dimension_semantics=("parallel", "parallel", "arbitrary"), vmem_limit_bytes=VMEM_LIMIT),
        name="ssd_scan",
    )(act, act, act, feat, ftc, d_row)


def _mlstm_kernel(q_ref, k_ref, v_ref, f_ref, ftc_ref, o_ref, c_ref, m_ref, e_ref, *, tb):
    d = pl.program_id(1)
    i = pl.program_id(2)

    @pl.when(i == 0)
    def _():
        c_ref[...] = jnp.zeros(c_ref.shape, F32)
        m_ref[...] = jnp.zeros(m_ref.shape, F32)

    ncb = tb // CHUNK
    dh = MLSTM_DH
    _expand_features(f_ref, e_ref, [(HW, F_IPRE, MLSTM_HEADS, dh), (HW, F_BCUM, MLSTM_HEADS, dh)], d)
    _, _, incl, _ = _dir_masks(d)
    tend = (CHUNK - 1) * (1 - d)
    ones = jnp.ones((CHUNK, dh), BF16)

    heads = range(MLSTM_HEADS)
    rows, bxs, bends, mcs = [], [], [], []
    qb, kb, kw, vaug, dmat, mis = [], [], [], [], [], []
    for cc in range(ncb):
        c = cc + d * (ncb - 1 - 2 * cc)
        r0 = pl.multiple_of(c * CHUNK, CHUNK)
        rows.append(r0)
        ix = e_ref[pl.ds(r0, CHUNK), 0:HW]
        bx = e_ref[pl.ds(r0, CHUNK), HW:2 * HW]
        bend = e_ref[pl.ds(r0 + tend, 1), HW:2 * HW]
        logw = bend - bx + ix
        mc = jnp.max(logw, axis=0, keepdims=True)
        ws = jnp.exp(logw - mc)
        bxs.append(bx)
        bends.append(bend)
        mcs.append(mc)
        q = q_ref[pl.ds(r0, CHUNK), :]
        k = k_ref[pl.ds(r0, CHUNK), :]
        kws = (k.astype(F32) * (ws * (dh ** -0.5))).astype(BF16)
        v = v_ref[pl.ds(r0, CHUNK), :]
        for h in heads:
            sl = slice(h * dh, (h + 1) * dh)
            qb.append(q[:, sl])
            kb.append(k[:, sl])
            kw.append(kws[:, sl])
            vaug.append(jnp.concatenate([v[:, sl], ones], axis=1))
            brow = ftc_ref[c, pl.ds(F_BCUM + d * MLSTM_HEADS + h, 1), :]
            irow = ftc_ref[c, pl.ds(F_IPRE + d * MLSTM_HEADS + h, 1), :]
            logd = jnp.where(incl, bx[:, h * dh:h * dh + CHUNK] - brow + irow, NEG)
            mi = jnp.max(logd, axis=-1, keepdims=True)
            mis.append(mi)
            dmat.append(jnp.exp(logd - mi) * (dh ** -0.5))
    qk = [_mm_nt(qi, ki) for qi, ki in zip(qb, kb)]
    kv = [_mm_tn(ki, vi) for ki, vi in zip(kw, vaug)]
    intra = [_mm(qi * di, vi) for qi, di, vi in zip(qk, dmat, vaug)]

    state = [c_ref[:, h * 2 * dh:(h + 1) * 2 * dh] for h in heads]
    m_old = m_ref[...]
    starts, m_starts = [], []
    for cc in range(ncb):
        starts.append([s.astype(BF16) for s in state])
        m_starts.append(m_old)
        m_new = jnp.maximum(bends[cc] + m_old, mcs[cc])
        s_old = jnp.exp(bends[cc] + m_old - m_new)
        s_add = jnp.exp(mcs[cc] - m_new)
        new_state = []
        for h in heads:
            sl = slice(h * dh, (h + 1) * dh)
            so = jnp.concatenate([s_old[:, sl], s_old[:, sl]], axis=1)
            sa = jnp.concatenate([s_add[:, sl], s_add[:, sl]], axis=1)
            new_state.append(state[h] * so + kv[cc * MLSTM_HEADS + h] * sa)
        state = new_state
        m_old = m_new
    for h in heads:
        c_ref[:, h * 2 * dh:(h + 1) * 2 * dh] = state[h]
    m_ref[...] = m_old
    inter = [_mm(qb[cc * MLSTM_HEADS + h], starts[cc][h]) for cc in range(ncb) for h in heads]
    for cc in range(ncb):
        m_inter = bxs[cc] + m_starts[cc]
        outs = []
        for h in heads:
            j = cc * MLSTM_HEADS + h
            mint = m_inter[:, h * dh:(h + 1) * dh]
            m_t = jnp.maximum(mint, mis[j])
            s_inter = jnp.exp(mint - m_t)
            s_intra = jnp.exp(mis[j] - m_t)
            num = s_inter * inter[j][:, 0:dh] + s_intra * intra[j][:, 0:dh]
            den = s_inter * inter[j][:, dh:2 * dh] + s_intra * intra[j][:, dh:2 * dh]
            outs.append(num / jnp.maximum(jnp.abs(den), jnp.exp(-m_t)))
        o_ref[pl.ds(rows[cc], CHUNK), :] = jnp.concatenate(outs, axis=1).astype(BF16)


def _mlstm(proj, feat, ftc, ns, seq):
    T = proj.shape[0]
    tb = TB_MIX
    nb = seq // tb
    rm = _mix_row_map(nb)
    kern = functools.partial(_mlstm_kernel, tb=tb)
    qkv = [pl.BlockSpec((tb, HW), lambda s, d, i, j=j: (rm(s, d, i), COL_CQKV // HW + j)) for j in range(3)]
    return pl.pallas_call(
        kern,
        grid=(ns, N_DIR, nb),
        in_specs=qkv + [
            pl.BlockSpec((tb, LANE), lambda s, d, i: (rm(s, d, i), 0)),
            pl.BlockSpec((tb // CHUNK, LANE, CHUNK), lambda s, d, i: (rm(s, d, i), 0, 0)),
        ],
        out_specs=pl.BlockSpec((None, tb, HW), lambda s, d, i: (d, rm(s, d, i), 0)),
        out_shape=jax.ShapeDtypeStruct((N_DIR, T, HW), BF16),
        scratch_shapes=[pltpu.VMEM((MLSTM_DH, 2 * HW), F32), pltpu.VMEM((1, HW), F32),
                        pltpu.VMEM((tb, 2 * HW), F32)],
        compiler_params=pltpu.CompilerParams(
            dimension_semantics=("parallel", "parallel", "arbitrary"), vmem_limit_bytes=VMEM_LIMIT),
        name="mlstm_scan",
    )(proj, proj, proj, feat, ftc)


def _post_kernel(og_ref, ys_ref, hm_ref, az_ref, sz_ref, co_ref, x_ref, wo_ref, nw_ref, ln_ref, o_ref):
    oa = og_ref[0].astype(F32) + og_ref[1].astype(F32)
    az = az_ref[...].astype(F32)
    parts = []
    for h in range(GDN_HEADS):
        sl = slice(h * GDN_DV, (h + 1) * GDN_DV)
        seg = oa[:, sl]
        ms = jnp.mean(seg * seg, -1, keepdims=True)
        parts.append(seg * lax.rsqrt(ms + NORM_EPS) * nw_ref[0:1, sl] * _silu(az[:, sl]))
    ys = (ys_ref[0].astype(F32) + ys_ref[1].astype(F32)) * _silu(sz_ref[...].astype(F32))
    gw = HW // SSD_GROUPS
    for g in range(SSD_GROUPS):
        sl = slice(g * gw, (g + 1) * gw)
        seg = ys[:, sl]
        ms = jnp.mean(seg * seg, -1, keepdims=True)
        parts.append(seg * lax.rsqrt(ms + NORM_EPS) * nw_ref[1:2, sl])
    hc = hm_ref[0].astype(F32) + hm_ref[1].astype(F32)
    co = co_ref[...].astype(F32)
    for h in range(MLSTM_HEADS):
        sl = slice(h * MLSTM_DH, (h + 1) * MLSTM_DH)
        seg = hc[:, sl]
        seg = seg - jnp.mean(seg, -1, keepdims=True)
        ms = jnp.mean(seg * seg, -1, keepdims=True)
        parts.append(_sigmoid(co[:, sl]) * (seg * lax.rsqrt(ms + NORM_EPS) * nw_ref[2:3, sl]))
    y = jnp.concatenate(parts, axis=1).astype(BF16)
    hmix = jnp.dot(y, wo_ref[...], preferred_element_type=F32)
    z = DN_ALPHA * x_ref[...] + hmix
    o_ref[...] = _layernorm_rows(z, ln_ref[0:1, :], ln_ref[1:2, :])


def _post(og, ys, hm, proj, x, w_out, norm_w, ln):
    T, D = x.shape
    tb = TB_POST
    dir_spec = pl.BlockSpec((N_DIR, tb, HW), lambda i: (0, i, 0))
    zc = COL_Z // HW
    return pl.pallas_call(
        _post_kernel,
        grid=(T // tb,),
        in_specs=[dir_spec, dir_spec, dir_spec,
                  pl.BlockSpec((tb, HW), lambda i: (i, zc)),
                  pl.BlockSpec((tb, HW), lambda i: (i, zc + 1)),
                  pl.BlockSpec((tb, HW), lambda i: (i, zc + 2)),
                  pl.BlockSpec((tb, D), lambda i: (i, 0)),
                  pl.BlockSpec(w_out.shape, lambda i: (0, 0)),
                  pl.BlockSpec((SUBLANE, HW), lambda i: (0, 0)),
                  pl.BlockSpec((SUBLANE, D), lambda i: (0, 0))],
        out_specs=pl.BlockSpec((tb, D), lambda i: (i, 0)),
        out_shape=jax.ShapeDtypeStruct((T, D), F32),
        compiler_params=pltpu.CompilerParams(dimension_semantics=("parallel",),
                                             vmem_limit_bytes=VMEM_LIMIT),
        name="post",
    )(og, ys, hm, proj, proj, proj, x, w_out, norm_w, ln)


def _swiglu_tile(xb, wg_ref, wu_ref, wd_ref, tf):
    acc = jnp.zeros((xb.shape[0], wd_ref.shape[1]), F32)
    for f0 in range(0, wg_ref.shape[1], tf):
        g = jnp.dot(xb, wg_ref[:, f0:f0 + tf], preferred_element_type=F32)
        u = jnp.dot(xb, wu_ref[:, f0:f0 + tf], preferred_element_type=F32)
        acc = acc + jnp.dot((_silu(g) * u).astype(BF16), wd_ref[f0:f0 + tf, :], preferred_element_type=F32)
    return acc


def _ffn_kernel(x_ref, wg_ref, wu_ref, wd_ref, ln_ref, o_ref, *, tf):
    x = x_ref[...]
    z = DN_ALPHA * x + _swiglu_tile(x.astype(BF16), wg_ref, wu_ref, wd_ref, tf)
    o_ref[...] = _layernorm_rows(z, ln_ref[0:1, :], ln_ref[1:2, :])


def _ffn(x, wg, wu, wd, ln):
    T, D = x.shape
    F = wg.shape[1]
    tm = TM_FFN
    kern = functools.partial(_ffn_kernel, tf=TF_FFN)
    resident = dict(pipeline_mode=pl.Buffered(1))
    return pl.pallas_call(
        kern,
        grid=(T // tm,),
        in_specs=[pl.BlockSpec((tm, D), lambda i: (i, 0)),
                  pl.BlockSpec((D, F), lambda i: (0, 0), **resident),
                  pl.BlockSpec((D, F), lambda i: (0, 0), **resident),
                  pl.BlockSpec((F, D), lambda i: (0, 0), **resident),
                  pl.BlockSpec((SUBLANE, D), lambda i: (0, 0))],
        out_specs=pl.BlockSpec((tm, D), lambda i: (i, 0)),
        out_shape=jax.ShapeDtypeStruct((T, D), F32),
        compiler_params=pltpu.CompilerParams(dimension_semantics=("parallel",),
                                             vmem_limit_bytes=VMEM_LIMIT),
        name="ffn",
    )(x, wg, wu, wd, ln)


I_E1, I_E2, I_R1, I_R2, I_G1, I_G2 = range(6)


def _route_kernel(x_ref, r_ref, info_ref, cnt_ref, run_ref, *, ne):
    @pl.when(pl.program_id(0) == 0)
    def _():
        run_ref[...] = jnp.zeros(run_ref.shape, F32)

    logits = jnp.dot(x_ref[...], r_ref[...], precision=HIGHEST, preferred_element_type=F32)
    tm = logits.shape[0]
    lane = lax.broadcasted_iota(jnp.int32, logits.shape, 1)
    lanef = lane.astype(F32)
    logits = jnp.where(lane < ne, logits, NEG)
    m1 = jnp.max(logits, -1, keepdims=True)
    i1 = jnp.min(jnp.where(logits == m1, lanef, float(LANE)), -1, keepdims=True)
    oh1 = lanef == i1
    rest = jnp.where(oh1, NEG, logits)
    m2 = jnp.max(rest, -1, keepdims=True)
    i2 = jnp.min(jnp.where(rest == m2, lanef, float(LANE)), -1, keepdims=True)
    oh2 = lanef == i2
    e2 = jnp.exp(m2 - m1)
    g1 = 1.0 / (1.0 + e2)
    g2 = e2 / (1.0 + e2)
    occ = jnp.where(oh1, 1.0, 0.0) + jnp.where(oh2, 1.0, 0.0)
    r = lax.broadcasted_iota(jnp.int32, (tm, tm), 0)
    c = lax.broadcasted_iota(jnp.int32, (tm, tm), 1)
    earlier = jnp.where(c < r, 1.0, 0.0).astype(BF16)
    base = run_ref[...] + jnp.dot(earlier, occ.astype(BF16), preferred_element_type=F32)
    rank1 = jnp.sum(jnp.where(oh1, base, 0.0), -1, keepdims=True)
    rank2 = jnp.sum(jnp.where(oh2, base, 0.0), -1, keepdims=True)
    total = run_ref[...] + jnp.sum(occ, axis=0, keepdims=True)
    run_ref[...] = total
    cnt_ref[...] = jnp.broadcast_to(total, cnt_ref.shape)
    info = jnp.zeros(logits.shape, F32)
    for idx, val in ((I_E1, i1), (I_E2, i2), (I_R1, rank1), (I_R2, rank2), (I_G1, g1), (I_G2, g2)):
        info = jnp.where(lane == idx, val, info)
    info_ref[...] = info


def _route(x, router):
    T, D = x.shape
    tm = TM_ROUTE
    kern = functools.partial(_route_kernel, ne=N_EXPERTS)
    return pl.pallas_call(
        kern,
        grid=(T // tm,),
        in_specs=[pl.BlockSpec((tm, D), lambda i: (i, 0)),
                  pl.BlockSpec((D, LANE), lambda i: (0, 0))],
        out_specs=[pl.BlockSpec((tm, LANE), lambda i: (i, 0)),
                   pl.BlockSpec((SUBLANE, LANE), lambda i: (0, 0))],
        out_shape=[jax.ShapeDtypeStruct((T, LANE), F32),
                   jax.ShapeDtypeStruct((SUBLANE, LANE), F32)],
        scratch_shapes=[pltpu.VMEM((1, LANE), F32)],
        compiler_params=pltpu.CompilerParams(dimension_semantics=("arbitrary",),
                                             vmem_limit_bytes=VMEM_LIMIT),
        name="route",
    )(x, router)


def _row_copy(src_ref, src_row, dst_ref, dst_row, sem):
    return pltpu.make_async_copy(src_ref.at[pl.ds(src_row, 1), :], dst_ref.at[pl.ds(dst_row, 1), :], sem)


def _for_rows(tm, fn):
    def body(g, carry):
        for k in range(ROW_UNROLL):
            fn(g * ROW_UNROLL + k, k)
        return carry

    lax.fori_loop(0, tm // ROW_UNROLL, body, 0)


def _dispatch_kernel(p1_ref, p2_ref, x_ref, xs_in_ref, xs_ref, sem, *, tm):
    del xs_in_ref

    def copies(r):
        return (_row_copy(x_ref, r, xs_ref, p1_ref[0, 0, r], sem),
                _row_copy(x_ref, r, xs_ref, p2_ref[0, 0, r], sem))

    def start(r, k):
        for j, cp in enumerate(copies(r)):
            cp.start(priority=(k + j) % 2)

    def wait(r, k):
        for cp in copies(r):
            cp.wait()

    _for_rows(tm, start)
    _for_rows(tm, wait)


def _dispatch(x, pos1, pos2, n_rows):
    T, D = x.shape
    tm = TM_ROUTE
    kern = functools.partial(_dispatch_kernel, tm=tm)
    pos_spec = pl.BlockSpec((1, 1, tm), lambda i: (i, 0, 0), memory_space=pltpu.SMEM)
    return pl.pallas_call(
        kern,
        grid=(T // tm,),
        in_specs=[pos_spec, pos_spec,
                  pl.BlockSpec((tm, D), lambda i: (i, 0)),
                  pl.BlockSpec(memory_space=pl.ANY)],
        out_specs=pl.BlockSpec(memory_space=pl.ANY),
        out_shape=jax.ShapeDtypeStruct((n_rows, D), F32),
        scratch_shapes=[pltpu.SemaphoreType.DMA(())],
        input_output_aliases={3: 0},
        compiler_params=pltpu.CompilerParams(dimension_semantics=("arbitrary",),
                                             vmem_limit_bytes=VMEM_LIMIT),
        name="dispatch",
    )(pos1, pos2, x, jnp.zeros((n_rows, D), F32))


def _gffn_kernel(te_ref, tv_ref, x_ref, wg_ref, wu_ref, wd_ref, y_ref, *, tf):
    del te_ref
    valid = tv_ref[pl.program_id(0)] == 1

    @pl.when(valid)
    def _():
        y_ref[...] = _swiglu_tile(x_ref[...].astype(BF16), wg_ref, wu_ref, wd_ref, tf)

    @pl.when(jnp.logical_not(valid))
    def _():
        y_ref[...] = jnp.zeros(y_ref.shape, F32)


def _gffn(xs, tile_expert, tile_valid, wg, wu, wd):
    P, D = xs.shape
    _, _, F = wg.shape
    tm = TM_GFFN
    kern = functools.partial(_gffn_kernel, tf=TF_MOE)
    grid_spec = pltpu.PrefetchScalarGridSpec(
        num_scalar_prefetch=2,
        grid=(P // tm,),
        in_specs=[pl.BlockSpec((tm, D), lambda i, te, tv: (i, 0)),
                  pl.BlockSpec((None, D, F), lambda i, te, tv: (te[i], 0, 0), pipeline_mode=pl.Buffered(1)),
                  pl.BlockSpec((None, D, F), lambda i, te, tv: (te[i], 0, 0), pipeline_mode=pl.Buffered(1)),
                  pl.BlockSpec((None, F, D), lambda i, te, tv: (te[i], 0, 0), pipeline_mode=pl.Buffered(1))],
        out_specs=pl.BlockSpec((tm, D), lambda i, te, tv: (i, 0)),
    )
    return pl.pallas_call(
        kern,
        grid_spec=grid_spec,
        out_shape=jax.ShapeDtypeStruct((P, D), F32),
        compiler_params=pltpu.CompilerParams(dimension_semantics=("arbitrary",),
                                             vmem_limit_bytes=VMEM_LIMIT),
        name="gffn",
    )(tile_expert, tile_valid, xs, wg, wu, wd)


def _combine_kernel(p1_ref, p2_ref, x_ref, info_ref, y_ref, ln_ref, o_ref, ya_ref, yb_ref, sem, *, tm):
    def copies(r):
        return (_row_copy(y_ref, p1_ref[0, 0, r], ya_ref, r, sem),
                _row_copy(y_ref, p2_ref[0, 0, r], yb_ref, r, sem))

    def start(r, k):
        for j, cp in enumerate(copies(r)):
            cp.start(priority=(k + j) % 2)

    def wait(r, k):
        for cp in copies(r):
            cp.wait()

    _for_rows(tm, start)
    _for_rows(tm, wait)
    info = info_ref[...]
    g1 = info[:, I_G1:I_G1 + 1]
    g2 = info[:, I_G2:I_G2 + 1]
    z = DN_ALPHA * x_ref[...] + (g1 * ya_ref[...] + g2 * yb_ref[...])
    o_ref[...] = _layernorm_rows(z, ln_ref[0:1, :], ln_ref[1:2, :])


def _combine(x, info, y, pos1, pos2, ln, row0, n_rows):
    _, D = x.shape
    tm = TM_ROUTE
    assert row0 % tm == 0 and n_rows % tm == 0
    t0 = row0 // tm
    kern = functools.partial(_combine_kernel, tm=tm)
    pos_spec = pl.BlockSpec((1, 1, tm), lambda i: (t0 + i, 0, 0), memory_space=pltpu.SMEM)
    return pl.pallas_call(
        kern,
        grid=(n_rows // tm,),
        in_specs=[pos_spec, pos_spec,
                  pl.BlockSpec((tm, D), lambda i: (t0 + i, 0)),
                  pl.BlockSpec((tm, LANE), lambda i: (t0 + i, 0)),
                  pl.BlockSpec(memory_space=pl.ANY),
                  pl.BlockSpec((SUBLANE, D), lambda i: (0, 0))],
        out_specs=pl.BlockSpec((tm, D), lambda i: (i, 0)),
        out_shape=jax.ShapeDtypeStruct((n_rows, D), F32),
        scratch_shapes=[pltpu.VMEM((tm, D), F32), pltpu.VMEM((tm, D), F32), pltpu.SemaphoreType.DMA(())],
        compiler_params=pltpu.CompilerParams(dimension_semantics=("arbitrary",),
                                             vmem_limit_bytes=VMEM_LIMIT),
        name="combine",
    )(pos1, pos2, x, info, y, ln)


def _moe(x, router, wg, wu, wd, ln, row_ranges):
    T, D = x.shape
    ne = wg.shape[0]
    tm = TM_GFFN
    n_tiles = (N_TOP * T) // tm + ne
    info, cnt = _route(x, router)
    counts = cnt[0, :ne].astype(jnp.int32)
    tiles_e = (counts + tm - 1) // tm
    tile_end = jnp.cumsum(tiles_e)
    offs = (tile_end - tiles_e) * tm
    tid = jnp.arange(n_tiles, dtype=jnp.int32)
    tile_expert = jnp.minimum(jnp.sum(tid[:, None] >= tile_end[None, :], axis=1), ne - 1).astype(jnp.int32)
    tile_valid = (tid < tile_end[-1]).astype(jnp.int32)
    eids = jnp.arange(ne, dtype=jnp.int32)[None, :]

    def slot(e_lane, r_lane):
        e = info[:, e_lane].astype(jnp.int32)
        off = jnp.sum(jnp.where(e[:, None] == eids, offs[None, :], 0), axis=1)
        return (off + info[:, r_lane].astype(jnp.int32)).reshape(T // TM_ROUTE, 1, TM_ROUTE)

    pos1 = slot(I_E1, I_R1)
    pos2 = slot(I_E2, I_R2)
    xs = _dispatch(x, pos1, pos2, n_tiles * tm)
    y = _gffn(xs, tile_expert, tile_valid, wg, wu, wd)
    return [_combine(x, info, y, pos1, pos2, ln, row0, n_rows) for row0, n_rows in row_ranges]


def _pad_rows(a, rows):
    return jnp.concatenate([a, jnp.zeros((rows - a.shape[0],) + a.shape[1:], a.dtype)], axis=0)


def _pack_w_in(w):
    gq = GDN_HEADS * GDN_DK
    gv = GDN_HEADS * GDN_DV
    sw = SSD_HEADS * SSD_HEADDIM
    sbc = SSD_GROUPS * SSD_STATE
    mw = MLSTM_HEADS * MLSTM_DH
    sizes = (gq, gq, gv, gv, N_DIR * GDN_HEADS, N_DIR * GDN_HEADS,
             sw, sw, sbc, sbc, N_DIR * SSD_HEADS,
             mw, mw, mw, mw, N_DIR * MLSTM_HEADS, N_DIR * MLSTM_HEADS)
    idx = np.cumsum(sizes)[:-1].tolist()
    (a_q, a_k, a_v, a_z, a_b, a_a, s_z, s_x, s_b, s_c, s_dt,
     c_q, c_k, c_v, c_o, c_i, c_f) = jnp.split(w, idx, axis=1)
    gates = jnp.concatenate([a_b, a_a, s_dt, s_dt, c_i, c_f], axis=1)
    gates = jnp.concatenate([gates, jnp.zeros((w.shape[0], LANE - gates.shape[1]), w.dtype)], axis=1)
    packed = jnp.concatenate([a_q, a_k, a_v, s_x, s_b, s_c, gates, c_q, c_k, c_v, a_z, s_z, c_o], axis=1)
    return packed.astype(BF16)


def _pack_gate_params(a_A_log, a_dt_bias, b_A_log, b_dt_bias, c_i_bias, c_f_bias):
    z8 = jnp.zeros((N_DIR * GDN_HEADS,), F32)
    z16 = jnp.zeros((N_DIR * SSD_HEADS,), F32)
    bias = jnp.concatenate([z8, a_dt_bias.reshape(-1), b_dt_bias.reshape(-1), b_dt_bias.reshape(-1),
                            c_i_bias.reshape(-1), c_f_bias.reshape(-1)])
    coef = jnp.concatenate([z8, -jnp.exp(a_A_log.reshape(-1)), z16, -jnp.exp(b_A_log.reshape(-1)), z8, z8])
    rows = jnp.stack([bias, coef]).astype(F32)
    rows = jnp.concatenate([rows, jnp.zeros((2, LANE - rows.shape[1]), F32)], axis=1)
    return _pad_rows(rows, SUBLANE)


def _mixer_layer(x, ns, seq, w_in, conv_a_w, a_A_log, a_dt_bias, a_norm_w, conv_b_w, conv_b_b,
                 b_A_log, b_dt_bias, b_D, b_norm_w, c_i_bias, c_f_bias, c_norm_w, w_out, ln_g, ln_b):
    proj, proj_h = _in_proj(x, _pack_w_in(w_in))
    conv_w = _pad_rows(jnp.concatenate([conv_a_w, conv_b_w], axis=1).astype(F32), SUBLANE)
    conv_b = jnp.concatenate([jnp.zeros((conv_a_w.shape[1],), F32), conv_b_b.astype(F32)])[None, :]
    gate_p = _pack_gate_params(a_A_log, a_dt_bias, b_A_log, b_dt_bias, c_i_bias, c_f_bias)
    act, feat, ftc = _prep(proj, conv_w, conv_b, gate_p, ns, seq)
    og = _gdn(act, feat, ftc, ns, seq)
    d_row = jnp.repeat(b_D.astype(F32), SSD_HEADDIM)[None, :]
    ys = _ssd(act, feat, ftc, d_row, ns, seq)
    hm = _mlstm(proj_h, feat, ftc, ns, seq)
    norm_w = _pad_rows(jnp.stack([jnp.tile(a_norm_w, GDN_HEADS), b_norm_w, c_norm_w]).astype(F32), SUBLANE)
    ln = _pad_rows(jnp.stack([ln_g, ln_b]).astype(F32), SUBLANE)
    return _post(og, ys, hm, proj_h, x, w_out.astype(BF16), norm_w, ln)


def kernel(x_prompt, x_sample, w_in, conv_a_w, a_A_log, a_dt_bias, a_norm_w, conv_b_w, conv_b_b, b_A_log, b_dt_bias, b_D, b_norm_w, c_i_bias, c_f_bias, c_norm_w, w_out, ln1_g, ln1_b, ln2_g, ln2_b, ffn_w_gate, ffn_w_up, ffn_w_down, moe_router, moe_w_gate, moe_w_up, moe_w_down):
    bp, seq, dm = x_prompt.shape
    bs, seq_s, _ = x_sample.shape
    assert seq == seq_s and dm == D_MODEL and w_in.shape[0] == DEPTH
    assert seq % TB_MIX == 0 and seq % TB_PREP == 0
    ns = bp + bs
    x = jnp.concatenate([x_prompt.reshape(bp * seq, dm), x_sample.reshape(bs * seq, dm)], axis=0)
    for l in range(DEPTH):
        x = _mixer_layer(x, ns, seq, w_in[l], conv_a_w[l], a_A_log[l], a_dt_bias[l], a_norm_w[l],
                         conv_b_w[l], conv_b_b[l], b_A_log[l], b_dt_bias[l], b_D[l], b_norm_w[l],
                         c_i_bias[l], c_f_bias[l], c_norm_w[l], w_out[l], ln1_g[l], ln1_b[l])
        ln2 = _pad_rows(jnp.stack([ln2_g[l], ln2_b[l]]).astype(F32), SUBLANE)
        j = l // 2
        if l % 2 == 0:
            x = _ffn(x, ffn_w_gate[j].astype(BF16), ffn_w_up[j].astype(BF16), ffn_w_down[j].astype(BF16), ln2)
        else:
            router = jnp.concatenate(
                [moe_router[j], jnp.zeros((dm, LANE - N_EXPERTS), moe_router.dtype)], axis=1).astype(F32)
            ranges = [(0, bp * seq), (bp * seq, bs * seq)] if l == DEPTH - 1 else [(0, ns * seq)]
            outs = _moe(x, router, moe_w_gate[j].astype(BF16), moe_w_up[j].astype(BF16),
                        moe_w_down[j].astype(BF16), ln2, ranges)
            if l == DEPTH - 1:
                return (outs[0].reshape(bp, seq, dm), outs[1].reshape(bs, seq, dm))
            x = outs[0]
    y_prompt = x[:bp * seq].reshape(bp, seq, dm)
    y_sample = x[bp * seq:].reshape(bs, seq, dm)
    return (y_prompt, y_sample)
```

```python
import functools

import jax
import jax.numpy as jnp
import numpy as np
from jax import lax
from jax.experimental import pallas as pl
from jax.experimental.pallas import tpu as pltpu

F32 = jnp.float32
BF16 = jnp.bfloat16
HIGHEST = lax.Precision.HIGHEST

D_MODEL = 1024
DEPTH = 2
GDN_HEADS, GDN_DK, GDN_DV = 4, 128, 128
SSD_HEADS, SSD_HEADDIM, SSD_GROUPS, SSD_STATE = 8, 64, 2, 128
MLSTM_HEADS, MLSTM_DH = 4, 128
N_DIR = 2
CONV_W = 4
CONV_PAD_L = (CONV_W - 1) // 2
CHUNK = 64
N_EXPERTS = 8
DN_ALPHA = (2 * DEPTH) ** 0.25
LN_EPS = 1e-5
NORM_EPS = 1e-6
NEG = -1e30

LANE = 128
SUBLANE = 8

HW = 512
COL_AQKV = 0
COL_SXBC = 3 * HW
N_CONV = 5 * HW
COL_GATE = N_CONV
N_PROJ_F32 = COL_GATE + LANE
COL_CQKV = 0
COL_Z = 3 * HW
N_PROJ_BF16 = 6 * HW

F_BETA, F_G, F_DT, F_ACS, F_IPRE, F_BCUM = 0, 8, 16, 32, 48, 56

TM_PROJ = 512
TB_PREP = 256
TB_MIX = 512
TB_POST = 512
TM_FFN = 512
TF_FFN = 256
TM_ROUTE = 512
TM_GFFN = 512
TF_MOE = 512
N_TOP = 2
ROW_UNROLL = 8
GDN_GROUPS = 2
VMEM_LIMIT = 56 * 1024 * 1024


def _sigmoid(x):
    return 1.0 / (1.0 + jnp.exp(-x))


def _silu(x):
    return x * _sigmoid(x)


def _softplus(x):
    return jnp.maximum(x, 0.0) + jnp.log(1.0 + jnp.exp(-jnp.abs(x)))


def _mm(a, b):
    return jnp.dot(a.astype(BF16), b.astype(BF16), preferred_element_type=F32)


def _mm_nt(a, b):
    return lax.dot_general(a.astype(BF16), b.astype(BF16), (((1,), (1,)), ((), ())),
                           preferred_element_type=F32)


def _mm_tn(a, b):
    return lax.dot_general(a.astype(BF16), b.astype(BF16), (((0,), (0,)), ((), ())),
                           preferred_element_type=F32)


def _layernorm_rows(z, g, b):
    mu = jnp.mean(z, -1, keepdims=True)
    zc = z - mu
    var = jnp.mean(zc * zc, -1, keepdims=True)
    return zc * lax.rsqrt(var + LN_EPS) * g + b


def _row_sources(xs, tm):
    specs, starts, t = [], [], 0
    for x in xs:
        n = x.shape[0] // tm
        assert n * tm == x.shape[0]
        specs.append(pl.BlockSpec((tm, x.shape[1]), lambda i, t=t, n=n: (jnp.clip(i - t, 0, n - 1), 0)))
        starts.append(t)
        t += n
    return specs, tuple(starts), t


def _pick_tile(i, refs, starts):
    x = refs[0][...]
    for r, s in zip(refs[1:], starts[1:]):
        x = jnp.where(i >= s, r[...], x)
    return x


def _inproj_kernel(*refs, starts):
    x_refs = refs[:len(starts)]
    w_ref, of_ref, oh_ref = refs[len(starts):]
    xb = _pick_tile(pl.program_id(0), x_refs, starts).astype(BF16)
    of_ref[...] = jnp.dot(xb, w_ref[:, 0:N_PROJ_F32], preferred_element_type=F32)
    oh_ref[...] = jnp.dot(xb, w_ref[:, N_PROJ_F32:N_PROJ_F32 + N_PROJ_BF16],
                          preferred_element_type=F32).astype(BF16)


def _in_proj(xs, w):
    D, N = w.shape
    assert N == N_PROJ_F32 + N_PROJ_BF16
    x_specs, starts, nt = _row_sources(xs, TM_PROJ)
    T = nt * TM_PROJ
    return pl.pallas_call(
        functools.partial(_inproj_kernel, starts=starts),
        grid=(nt,),
        in_specs=x_specs + [pl.BlockSpec((D, N), lambda i: (0, 0), pipeline_mode=pl.Buffered(1))],
        out_specs=[pl.BlockSpec((TM_PROJ, N_PROJ_F32), lambda i: (i, 0)),
                   pl.BlockSpec((TM_PROJ, N_PROJ_BF16), lambda i: (i, 0))],
        out_shape=[jax.ShapeDtypeStruct((T, N_PROJ_F32), F32),
                   jax.ShapeDtypeStruct((T, N_PROJ_BF16), BF16)],
        compiler_params=pltpu.CompilerParams(dimension_semantics=("parallel",),
                                             vmem_limit_bytes=VMEM_LIMIT),
        name="in_proj",
    )(*xs, w)


def _prep_kernel(xm_ref, xp_ref, xn_ref, g_ref, cw_ref, cb_ref, gp_ref,
                 act_ref, f_ref, ftc_ref, pad_ref, *, tb, nb):
    b = pl.program_id(1)
    has_prev = (b > 0).astype(F32)
    has_next = (b < nb - 1).astype(F32)
    pad_ref[0:SUBLANE, :] = xp_ref[...] * has_prev
    pad_ref[SUBLANE:SUBLANE + tb, :] = xm_ref[...]
    pad_ref[SUBLANE + tb:2 * SUBLANE + tb, :] = xn_ref[...] * has_next

    base = SUBLANE - CONV_PAD_L
    for c0 in range(0, N_CONV, HW):
        y = cb_ref[:, c0:c0 + HW]
        for j in range(CONV_W):
            y = y + pad_ref[base + j:base + j + tb, c0:c0 + HW] * cw_ref[j:j + 1, c0:c0 + HW]
        y = _silu(y)
        if c0 in (COL_AQKV, COL_AQKV + HW):
            scale = GDN_DK ** -0.5 if c0 == COL_AQKV else 1.0
            parts = []
            for h in range(GDN_HEADS):
                seg = y[:, h * GDN_DK:(h + 1) * GDN_DK]
                ss = jnp.sum(seg * seg, -1, keepdims=True)
                parts.append(seg * lax.rsqrt(ss + NORM_EPS) * scale)
            y = jnp.concatenate(parts, axis=1)
        act_ref[:, c0:c0 + HW] = y.astype(BF16)

    p = g_ref[...]
    z = p + gp_ref[0:1, :]
    coef = gp_ref[1:2, :]
    lane = lax.broadcasted_iota(jnp.int32, p.shape, 1)
    sp = _softplus(z)
    val = jnp.where(lane < F_G, _sigmoid(p),
          jnp.where(lane < F_DT, coef * sp,
          jnp.where(lane < F_ACS, sp,
          jnp.where(lane < F_IPRE, coef * sp,
          jnp.where(lane < F_BCUM, z, -_softplus(-z))))))
    is_cum = ((lane >= F_G) & (lane < F_DT)) | ((lane >= F_ACS) & (lane < F_IPRE)) | (
        (lane >= F_BCUM) & (lane < F_BCUM + 8))
    wide = (lane >= F_DT) & (lane < F_IPRE)
    col_bwd = jnp.where(wide, lane % 16, 2 * (lane % 8)) >= 8
    r = lax.broadcasted_iota(jnp.int32, (tb, tb), 0)
    c = lax.broadcasted_iota(jnp.int32, (tb, tb), 1)
    same = (r // CHUNK) == (c // CHUNK)
    tril = jnp.where(same & (c <= r), 1.0, 0.0).astype(F32)
    triu = jnp.where(same & (c >= r), 1.0, 0.0).astype(F32)
    cs_f = jnp.dot(tril, val, precision=HIGHEST, preferred_element_type=F32)
    cs_b = jnp.dot(triu, val, precision=HIGHEST, preferred_element_type=F32)
    feat = jnp.where(is_cum, jnp.where(col_bwd, cs_b, cs_f), val)
    f_ref[...] = feat
    for k in range(tb // CHUNK):
        ftc_ref[k] = feat[k * CHUNK:(k + 1) * CHUNK, :].T


def _prep(proj, conv_w, conv_b, gate_p, ns, seq):
    T = proj.shape[0]
    tb = TB_PREP
    nb = seq // tb
    t8 = tb // SUBLANE
    n8 = T // SUBLANE
    kern = functools.partial(_prep_kernel, tb=tb, nb=nb)
    return pl.pallas_call(
        kern,
        grid=(ns, nb),
        in_specs=[
            pl.BlockSpec((tb, N_CONV), lambda s, b: (s * nb + b, 0)),
            pl.BlockSpec((SUBLANE, N_CONV), lambda s, b: (jnp.maximum((s * nb + b) * t8 - 1, 0), 0)),
            pl.BlockSpec((SUBLANE, N_CONV), lambda s, b: (jnp.minimum((s * nb + b + 1) * t8, n8 - 1), 0)),
            pl.BlockSpec((tb, LANE), lambda s, b: (s * nb + b, COL_GATE // LANE)),
            pl.BlockSpec((SUBLANE, N_CONV), lambda s, b: (0, 0)),
            pl.BlockSpec((1, N_CONV), lambda s, b: (0, 0)),
            pl.BlockSpec((SUBLANE, LANE), lambda s, b: (0, 0)),
        ],
        out_specs=[
            pl.BlockSpec((tb, N_CONV), lambda s, b: (s * nb + b, 0)),
            pl.BlockSpec((tb, LANE), lambda s, b: (s * nb + b, 0)),
            pl.BlockSpec((tb // CHUNK, LANE, CHUNK), lambda s, b: (s * nb + b, 0, 0)),
        ],
        out_shape=[
            jax.ShapeDtypeStruct((T, N_CONV), BF16),
            jax.ShapeDtypeStruct((T, LANE), F32),
            jax.ShapeDtypeStruct((T // CHUNK, LANE, CHUNK), F32),
        ],
        scratch_shapes=[pltpu.VMEM((tb + 2 * SUBLANE, N_CONV), F32)],
        compiler_params=pltpu.CompilerParams(dimension_semantics=("parallel", "parallel"),
                                             vmem_limit_bytes=VMEM_LIMIT),
        name="prep",
    )(proj, proj, proj, proj, conv_w, conv_b, gate_p)


def _expand_features(f_ref, e_ref, targets, d):
    n = e_ref.shape[1]
    crow = lax.broadcasted_iota(jnp.int32, (LANE, n), 0)
    lcol = lax.broadcasted_iota(jnp.int32, (LANE, n), 1)
    tgt = jnp.zeros((LANE, n), jnp.int32)
    off = 0
    for width, base, heads, per in targets:
        seg = base + d * heads + (lcol - off) // per
        tgt = jnp.where((lcol >= off) & (lcol < off + width), seg, tgt)
        off += width
    sel = jnp.where(crow == tgt, 1.0, 0.0).astype(BF16)
    f = f_ref[...]
    hi = f.astype(BF16)
    r1 = f - hi.astype(F32)
    mid = r1.astype(BF16)
    lo = (r1 - mid.astype(F32)).astype(BF16)
    e_ref[...] = (jnp.dot(jnp.concatenate([hi, mid], axis=1), jnp.concatenate([sel, sel], axis=0),
                          preferred_element_type=F32)
                  + jnp.dot(lo, sel, preferred_element_type=F32))


def _dir_masks(d):
    row = lax.broadcasted_iota(jnp.int32, (CHUNK, CHUNK), 0)
    col = lax.broadcasted_iota(jnp.int32, (CHUNK, CHUNK), 1)
    diff = (row - col) * (1 - 2 * d)
    return row, col, diff >= 0, diff > 0


def _mix_row_map(nb):
    return lambda s, d, i: s * nb + i + d * (nb - 1 - 2 * i)


def _tri_inverse_all(a_list, eye, m8, m16, m32, m64):
    a8 = [jnp.where(m8, a, 0.0) for a in a_list]
    x = [eye - a for a in a8]
    a8 = [a.astype(BF16) for a in a8]
    p2 = [_mm(a, a).astype(BF16) for a in a8]
    yield
    x = [xi + _mm(xi, pi) for xi, pi in zip(x, p2)]
    p4 = [_mm(pi, pi) for pi in p2]
    yield
    x = [xi + _mm(xi, pi) for xi, pi in zip(x, p4)]
    yield
    for m in (m16, m32, m64):
        xb = [t.astype(BF16) for t in x]
        y = [_mm(jnp.where(m, a, 0.0), xi) for a, xi in zip(a_list, xb)]
        yield
        x = [xf - _mm(xi, yi) for xf, xi, yi in zip(x, xb, y)]
        yield
    return x


def _drain(*gens):
    live = list(gens)
    while live:
        for g in list(live):
            try:
                next(g)
            except StopIteration:
                live.remove(g)


def _gdn_kernel(q_ref, k_ref, v_ref, f_ref, ftc_ref, o_ref, s_ref, e_ref, *, tb):
    d = pl.program_id(1)
    i = pl.program_id(2)

    @pl.when(i == 0)
    def _():
        s_ref[...] = jnp.zeros(s_ref.shape, F32)

    ncb = tb // CHUNK
    _expand_features(f_ref, e_ref, [(HW, F_BETA, GDN_HEADS, GDN_DV), (HW, F_G, GDN_HEADS, GDN_DV)], d)
    row, col, incl, strict = _dir_masks(d)
    eye = jnp.where(row == col, 1.0, 0.0).astype(F32)
    b8 = (row // 8) == (col // 8)
    b16 = (row // 16) == (col // 16)
    b32 = (row // 32) == (col // 32)
    m16 = b16 & jnp.logical_not(b8)
    m32 = b32 & jnp.logical_not(b16)
    m64 = jnp.logical_not(b32)
    tend = (CHUNK - 1) * (1 - d)

    heads = range(GDN_HEADS)

    def prepare(chunks, out):
        rows, ges = [], []
        qb, kb, rhs, qde, kde, bcol, dec = [], [], [], [], [], [], []
        for cc in chunks:
            c = cc + d * (ncb - 1 - 2 * cc)
            r0 = pl.multiple_of(c * CHUNK, CHUNK)
            rows.append(r0)
            bx = e_ref[pl.ds(r0, CHUNK), 0:HW]
            gx = e_ref[pl.ds(r0, CHUNK), HW:2 * HW]
            gend = e_ref[pl.ds(r0 + tend, 1), HW:2 * HW]
            eg = jnp.exp(gx)
            kdf = jnp.exp(gend - gx)
            ges.append(jnp.exp(gend))
            q16 = q_ref[pl.ds(r0, CHUNK), :]
            k16 = k_ref[pl.ds(r0, CHUNK), :]
            q = q16.astype(F32)
            k = k16.astype(F32)
            v = v_ref[pl.ds(r0, CHUNK), :].astype(F32)
            kbeta = k * bx
            vb = (v * bx).astype(BF16)
            kbe = (kbeta * eg).astype(BF16)
            qd = (q * eg).astype(BF16)
            kd = (k * kdf).astype(BF16)
            for h in heads:
                sl = slice(h * GDN_DV, (h + 1) * GDN_DV)
                qb.append(q16[:, sl])
                kb.append(k16[:, sl])
                rhs.append(jnp.concatenate([vb[:, sl], kbe[:, sl]], axis=1))
                qde.append(qd[:, sl])
                kde.append(kd[:, sl])
                bcol.append(bx[:, h * GDN_DV:h * GDN_DV + CHUNK])
                grow = ftc_ref[c, pl.ds(F_G + d * GDN_HEADS + h, 1), :]
                gd = gx[:, h * GDN_DV:h * GDN_DV + CHUNK] - grow
                dec.append(jnp.exp(jnp.where(incl, gd, NEG)))
        qkk = [_mm_nt(jnp.concatenate([qi, ki], axis=0), ki) for qi, ki in zip(qb, kb)]
        yield
        a = [jnp.where(strict, bi * t[CHUNK:2 * CHUNK] * di, 0.0) for bi, t, di in zip(bcol, qkk, dec)]
        tinv = yield from _tri_inverse_all(a, eye, b8, m16, m32, m64)
        uw = [_mm(ti, ri) for ti, ri in zip(tinv, rhs)]
        yield
        qkd = [(t[0:CHUNK] * di).astype(BF16) for t, di in zip(qkk, dec)]
        out.update(rows=rows, ges=ges, qde=qde, kde=kde, uw=uw, qkd=qkd)

    state = [s_ref[:, h * GDN_DV:(h + 1) * GDN_DV] for h in heads]

    def recur(pre):
        for n, r0 in enumerate(pre["rows"]):
            p0 = n * GDN_HEADS
            uw, qde, kde, qkd = pre["uw"], pre["qde"], pre["kde"], pre["qkd"]
            sb = [s.astype(BF16) for s in state]
            wq = [_mm(jnp.concatenate([uw[p0 + h][:, GDN_DV:2 * GDN_DV].astype(BF16), qde[p0 + h]], axis=0),
                      sb[h]) for h in heads]
            yield
            v_new = [(uw[p0 + h][:, 0:GDN_DV] - wq[h][0:CHUNK]).astype(BF16) for h in heads]
            outs = [wq[h][CHUNK:2 * CHUNK] + _mm(qkd[p0 + h], v_new[h]) for h in heads]
            for h in heads:
                state[h] = (state[h] * pre["ges"][n][:, h * GDN_DV:(h + 1) * GDN_DV]
                            + _mm_tn(kde[p0 + h], v_new[h]))
            yield
            o_ref[pl.ds(r0, CHUNK), :] = jnp.concatenate(outs, axis=1).astype(BF16)

    gsz = max(ncb // GDN_GROUPS, 1)
    pres = [{} for _ in range(0, ncb, gsz)]
    _drain(prepare(range(0, gsz), pres[0]))
    for n in range(1, len(pres)):
        _drain(prepare(range(n * gsz, (n + 1) * gsz), pres[n]), recur(pres[n - 1]))
    _drain(recur(pres[-1]))
    for h in heads:
        s_ref[:, h * GDN_DV:(h + 1) * GDN_DV] = state[h]


def _gdn(act, feat, ftc, ns, seq):
    T = act.shape[0]
    tb = TB_MIX
    nb = seq // tb
    rm = _mix_row_map(nb)
    kern = functools.partial(_gdn_kernel, tb=tb)
    qkv = [pl.BlockSpec((tb, HW), lambda s, d, i, j=j: (rm(s, d, i), j)) for j in range(3)]
    return pl.pallas_call(
        kern,
        grid=(ns, N_DIR, nb),
        in_specs=qkv + [
            pl.BlockSpec((tb, LANE), lambda s, d, i: (rm(s, d, i), 0)),
            pl.BlockSpec((tb // CHUNK, LANE, CHUNK), lambda s, d, i: (rm(s, d, i), 0, 0)),
        ],
        out_specs=pl.BlockSpec((None, tb, HW), lambda s, d, i: (d, rm(s, d, i), 0)),
        out_shape=jax.ShapeDtypeStruct((N_DIR, T, HW), BF16),
        scratch_shapes=[pltpu.VMEM((GDN_DK, HW), F32), pltpu.VMEM((tb, 2 * HW), F32)],
        compiler_params=pltpu.CompilerParams(
            dimension_semantics=("parallel", "parallel", "arbitrary"), vmem_limit_bytes=VMEM_LIMIT),
        name="gdn_scan",
    )(act, act, act, feat, ftc)


def _ssd_kernel(x_ref, b_ref, c_ref, f_ref, ftc_ref, dp_ref, o_ref, h_ref, e_ref, *, tb):
    d = pl.program_id(1)
    i = pl.program_id(2)

    @pl.when(i == 0)
    def _():
        h_ref[...] = jnp.zeros(h_ref.shape, F32)

    ncb = tb // CHUNK
    _expand_features(f_ref, e_ref, [(HW, F_DT, SSD_HEADS, SSD_HEADDIM), (HW, F_ACS, SSD_HEADS, SSD_HEADDIM)], d)
    tend = (CHUNK - 1) * (1 - d)
    row = lax.broadcasted_iota(jnp.int32, (CHUNK, LANE), 0)
    lane = lax.broadcasted_iota(jnp.int32, (CHUNK, LANE), 1)
    incl = (row - lane % CHUNK) * (1 - 2 * d) >= 0
    lo = lane < SSD_HEADDIM
    gw = HW // SSD_GROUPS
    spg = gw // LANE
    dfac = (1 - d).astype(F32)

    groups = range(SSD_GROUPS)
    rows, xs, eas, cds, cgs, bgs, xdtes, xhs, lms = [], [], [], [], [], [], [], [], []
    for cc in range(ncb):
        c = cc + d * (ncb - 1 - 2 * cc)
        r0 = pl.multiple_of(c * CHUNK, CHUNK)
        rows.append(r0)
        dtx = e_ref[pl.ds(r0, CHUNK), 0:HW]
        ax = e_ref[pl.ds(r0, CHUNK), HW:2 * HW]
        aend = e_ref[pl.ds(r0 + tend, 1), HW:2 * HW]
        x = x_ref[pl.ds(r0, CHUNK), :].astype(F32)
        xdt = x * dtx
        xdte = (xdt * jnp.exp(aend - ax)).astype(BF16)
        xs.append(x)
        eas.append(jnp.exp(ax))
        cds.append(jnp.exp(aend))
        bm = b_ref[pl.ds(r0, CHUNK), :]
        cm = c_ref[pl.ds(r0, CHUNK), :]
        for g in groups:
            bgs.append(bm[:, g * SSD_STATE:(g + 1) * SSD_STATE])
            cgs.append(cm[:, g * SSD_STATE:(g + 1) * SSD_STATE])
            xdtes.append(xdte[:, g * gw:(g + 1) * gw])
        for s in range(HW // LANE):
            xp = xdt[:, s * LANE:(s + 1) * LANE].astype(BF16)
            zero = jnp.zeros_like(xp)
            xhs.append(jnp.concatenate([jnp.where(lo, xp, zero), jnp.where(lo, zero, xp)], axis=0))
            arow = jnp.concatenate(
                [ftc_ref[c, pl.ds(F_ACS + d * SSD_HEADS + 2 * s, 1), :],
                 ftc_ref[c, pl.ds(F_ACS + d * SSD_HEADS + 2 * s + 1, 1), :]], axis=1)
            lms.append(jnp.exp(jnp.where(incl, ax[:, s * LANE:(s + 1) * LANE] - arow, NEG)))
    cbs = [_mm_nt(cg, jnp.concatenate([bg, bg], axis=0)) for cg, bg in zip(cgs, bgs)]
    sts = [_mm_tn(bg, xe) for bg, xe in zip(bgs, xdtes)]
    nslab = HW // LANE
    yds = [_mm(cbs[(j // nslab) * SSD_GROUPS + (j % nslab) // spg] * lms[j], xhs[j])
           for j in range(ncb * nslab)]

    state = [h_ref[:, g * gw:(g + 1) * gw] for g in groups]
    starts = []
    for cc in range(ncb):
        starts.append([s.astype(BF16) for s in state])
        state = [state[g] * cds[cc][:, g * gw:(g + 1) * gw] + sts[cc * SSD_GROUPS + g] for g in groups]
    for g in groups:
        h_ref[:, g * gw:(g + 1) * gw] = state[g]
    for cc in range(ncb):
        outs = []
        for g in groups:
            y_g = _mm(cgs[cc * SSD_GROUPS + g], starts[cc][g]) * eas[cc][:, g * gw:(g + 1) * gw]
            j0 = cc * nslab + g * spg
            outs.append(y_g + jnp.concatenate(yds[j0:j0 + spg], axis=1))
        y = jnp.concatenate(outs, axis=1) + xs[cc] * (dp_ref[...] * dfac)
        o_ref[pl.ds(rows[cc], CHUNK), :] = y.astype(BF16)


def _ssd(act, feat, ftc, d_row, ns, seq):
    T = act.shape[0]
    tb = TB_MIX
    nb = seq // tb
    rm = _mix_row_map(nb)
    kern = functools.partial(_ssd_kernel, tb=tb)
    bc_w = SSD_GROUPS * SSD_STATE
    return pl.pallas_call(
        kern,
        grid=(ns, N_DIR, nb),
        in_specs=[
            pl.BlockSpec((tb, HW), lambda s, d, i: (rm(s, d, i), COL_SXBC // HW)),
            pl.BlockSpec((tb, bc_w), lambda s, d, i: (rm(s, d, i), (COL_SXBC + HW) // bc_w)),
            pl.BlockSpec((tb, bc_w), lambda s, d, i: (rm(s, d, i), (COL_SXBC + HW) // bc_w + 1)),
            pl.BlockSpec((tb, LANE), lambda s, d, i: (rm(s, d, i), 0)),
            pl.BlockSpec((tb // CHUNK, LANE, CHUNK), lambda s, d, i: (rm(s, d, i), 0, 0)),
            pl.BlockSpec((1, HW), lambda s, d, i: (0, 0)),
        ],
        out_specs=pl.BlockSpec((None, tb, HW), lambda s, d, i: (d, rm(s, d, i), 0)),
        out_shape=jax.ShapeDtypeStruct((N_DIR, T, HW), BF16),
        scratch_shapes=[pltpu.VMEM((SSD_STATE, HW), F32), pltpu.VMEM((tb, 2 * HW), F32)],
        compiler_params=pltpu.CompilerParams(
            dimension_semantics=("parallel", "parallel", "arbitrary"), vmem_limit_bytes=VMEM_LIMIT),
        name="ssd_scan",
    )(act, act, act, feat, ftc, d_row)


def _mlstm_kernel(q_ref, k_ref, v_ref, f_ref, ftc_ref, o_ref, c_ref, m_ref, e_ref, *, tb):
    d = pl.program_id(1)
    i = pl.program_id(2)

    @pl.when(i == 0)
    def _():
        c_ref[...] = jnp.zeros(c_ref.shape, F32)
        m_ref[...] = jnp.zeros(m_ref.shape, F32)

    ncb = tb // CHUNK
    dh = MLSTM_DH
    _expand_features(f_ref, e_ref, [(HW, F_IPRE, MLSTM_HEADS, dh), (HW, F_BCUM, MLSTM_HEADS, dh)], d)
    _, _, incl, _ = _dir_masks(d)
    tend = (CHUNK - 1) * (1 - d)
    ones = jnp.ones((CHUNK, dh), BF16)

    heads = range(MLSTM_HEADS)
    rows, bxs, bends, mcs = [], [], [], []
    qb, kb, kw, vaug, dmat, mis = [], [], [], [], [], []
    for cc in range(ncb):
        c = cc + d * (ncb - 1 - 2 * cc)
        r0 = pl.multiple_of(c * CHUNK, CHUNK)
        rows.append(r0)
        ix = e_ref[pl.ds(r0, CHUNK), 0:HW]
        bx = e_ref[pl.ds(r0, CHUNK), HW:2 * HW]
        bend = e_ref[pl.ds(r0 + tend, 1), HW:2 * HW]
        logw = bend - bx + ix
        mc = jnp.max(logw, axis=0, keepdims=True)
        ws = jnp.exp(logw - mc)
        bxs.append(bx)
        bends.append(bend)
        mcs.append(mc)
        q = q_ref[pl.ds(r0, CHUNK), :]
        k = k_ref[pl.ds(r0, CHUNK), :]
        kws = (k.astype(F32) * (ws * (dh ** -0.5))).astype(BF16)
        v = v_ref[pl.ds(r0, CHUNK), :]
        for h in heads:
            sl = slice(h * dh, (h + 1) * dh)
            qb.append(q[:, sl])
            kb.append(k[:, sl])
            kw.append(kws[:, sl])
            vaug.append(jnp.concatenate([v[:, sl], ones], axis=1))
            brow = ftc_ref[c, pl.ds(F_BCUM + d * MLSTM_HEADS + h, 1), :]
            irow = ftc_ref[c, pl.ds(F_IPRE + d * MLSTM_HEADS + h, 1), :]
            logd = jnp.where(incl, bx[:, h * dh:h * dh + CHUNK] - brow + irow, NEG)
            mi = jnp.max(logd, axis=-1, keepdims=True)
            mis.append(mi)
            dmat.append(jnp.exp(logd - mi) * (dh ** -0.5))
    qk = [_mm_nt(qi, ki) for qi, ki in zip(qb, kb)]
    kv = [_mm_tn(ki, vi) for ki, vi in zip(kw, vaug)]
    intra = [_mm(qi * di, vi) for qi, di, vi in zip(qk, dmat, vaug)]

    state = [c_ref[:, h * 2 * dh:(h + 1) * 2 * dh] for h in heads]
    m_old = m_ref[...]
    starts, m_starts = [], []
    for cc in range(ncb):
        starts.append([s.astype(BF16) for s in state])
        m_starts.append(m_old)
        m_new = jnp.maximum(bends[cc] + m_old, mcs[cc])
        s_old = jnp.exp(bends[cc] + m_old - m_new)
        s_add = jnp.exp(mcs[cc] - m_new)
        new_state = []
        for h in heads:
            sl = slice(h * dh, (h + 1) * dh)
            so = jnp.concatenate([s_old[:, sl], s_old[:, sl]], axis=1)
            sa = jnp.concatenate([s_add[:, sl], s_add[:, sl]], axis=1)
            new_state.append(state[h] * so + kv[cc * MLSTM_HEADS + h] * sa)
        state = new_state
        m_old = m_new
    for h in heads:
        c_ref[:, h * 2 * dh:(h + 1) * 2 * dh] = state[h]
    m_ref[...] = m_old
    inter = [_mm(qb[cc * MLSTM_HEADS + h], starts[cc][h]) for cc in range(ncb) for h in heads]
    for cc in range(ncb):
        m_inter = bxs[cc] + m_starts[cc]
        outs = []
        for h in heads:
            j = cc * MLSTM_HEADS + h
            mint = m_inter[:, h * dh:(h + 1) * dh]
            m_t = jnp.maximum(mint, mis[j])
            s_inter = jnp.exp(mint - m_t)
            s_intra = jnp.exp(mis[j] - m_t)
            num = s_inter * inter[j][:, 0:dh] + s_intra * intra[j][:, 0:dh]
            den = s_inter * inter[j][:, dh:2 * dh] + s_intra * intra[j][:, dh:2 * dh]
            outs.append(num / jnp.maximum(jnp.abs(den), jnp.exp(-m_t)))
        o_ref[pl.ds(rows[cc], CHUNK), :] = jnp.concatenate(outs, axis=1).astype(BF16)


def _mlstm(proj, feat, ftc, ns, seq):
    T = proj.shape[0]
    tb = TB_MIX
    nb = seq // tb
    rm = _mix_row_map(nb)
    kern = functools.partial(_mlstm_kernel, tb=tb)
    qkv = [pl.BlockSpec((tb, HW), lambda s, d, i, j=j: (rm(s, d, i), COL_CQKV // HW + j)) for j in range(3)]
    return pl.pallas_call(
        kern,
        grid=(ns, N_DIR, nb),
        in_specs=qkv + [
            pl.BlockSpec((tb, LANE), lambda s, d, i: (rm(s, d, i), 0)),
            pl.BlockSpec((tb // CHUNK, LANE, CHUNK), lambda s, d, i: (rm(s, d, i), 0, 0)),
        ],
        out_specs=pl.BlockSpec((None, tb, HW), lambda s, d, i: (d, rm(s, d, i), 0)),
        out_shape=jax.ShapeDtypeStruct((N_DIR, T, HW), BF16),
        scratch_shapes=[pltpu.VMEM((MLSTM_DH, 2 * HW), F32), pltpu.VMEM((1, HW), F32),
                        pltpu.VMEM((tb, 2 * HW), F32)],
        compiler_params=pltpu.CompilerParams(
            dimension_semantics=("parallel", "parallel", "arbitrary"), vmem_limit_bytes=VMEM_LIMIT),
        name="mlstm_scan",
    )(proj, proj, proj, feat, ftc)


def _post_kernel(*refs, starts):
    x_refs = refs[:len(starts)]
    og_ref, ys_ref, hm_ref, az_ref, sz_ref, co_ref, wo_ref, nw_ref, ln_ref, o_ref = refs[len(starts):]
    oa = og_ref[0].astype(F32) + og_ref[1].astype(F32)
    az = az_ref[...].astype(F32)
    parts = []
    for h in range(GDN_HEADS):
        sl = slice(h * GDN_DV, (h + 1) * GDN_DV)
        seg = oa[:, sl]
        ms = jnp.mean(seg * seg, -1, keepdims=True)
        parts.append(seg * lax.rsqrt(ms + NORM_EPS) * nw_ref[0:1, sl] * _silu(az[:, sl]))
    ys = (ys_ref[0].astype(F32) + ys_ref[1].astype(F32)) * _silu(sz_ref[...].astype(F32))
    gw = HW // SSD_GROUPS
    for g in range(SSD_GROUPS):
        sl = slice(g * gw, (g + 1) * gw)
        seg = ys[:, sl]
        ms = jnp.mean(seg * seg, -1, keepdims=True)
        parts.append(seg * lax.rsqrt(ms + NORM_EPS) * nw_ref[1:2, sl])
    hc = hm_ref[0].astype(F32) + hm_ref[1].astype(F32)
    co = co_ref[...].astype(F32)
    for h in range(MLSTM_HEADS):
        sl = slice(h * MLSTM_DH, (h + 1) * MLSTM_DH)
        seg = hc[:, sl]
        seg = seg - jnp.mean(seg, -1, keepdims=True)
        ms = jnp.mean(seg * seg, -1, keepdims=True)
        parts.append(_sigmoid(co[:, sl]) * (seg * lax.rsqrt(ms + NORM_EPS) * nw_ref[2:3, sl]))
    y = jnp.concatenate(parts, axis=1).astype(BF16)
    hmix = jnp.dot(y, wo_ref[...], preferred_element_type=F32)
    z = DN_ALPHA * _pick_tile(pl.program_id(0), x_refs, starts) + hmix
    o_ref[...] = _layernorm_rows(z, ln_ref[0:1, :], ln_ref[1:2, :])


def _post(og, ys, hm, proj, xs, w_out, norm_w, ln):
    D = w_out.shape[1]
    tb = TB_POST
    x_specs, starts, nt = _row_sources(xs, tb)
    T = nt * tb
    dir_spec = pl.BlockSpec((N_DIR, tb, HW), lambda i: (0, i, 0))
    zc = COL_Z // HW
    return pl.pallas_call(
        functools.partial(_post_kernel, starts=starts),
        grid=(nt,),
        in_specs=x_specs + [dir_spec, dir_spec, dir_spec,
                            pl.BlockSpec((tb, HW), lambda i: (i, zc)),
                            pl.BlockSpec((tb, HW), lambda i: (i, zc + 1)),
                            pl.BlockSpec((tb, HW), lambda i: (i, zc + 2)),
                            pl.BlockSpec(w_out.shape, lambda i: (0, 0)),
                            pl.BlockSpec((SUBLANE, HW), lambda i: (0, 0)),
                            pl.BlockSpec((SUBLANE, D), lambda i: (0, 0))],
        out_specs=pl.BlockSpec((tb, D), lambda i: (i, 0)),
        out_shape=jax.ShapeDtypeStruct((T, D), F32),
        compiler_params=pltpu.CompilerParams(dimension_semantics=("parallel",),
                                             vmem_limit_bytes=VMEM_LIMIT),
        name="post",
    )(*xs, og, ys, hm, proj, proj, proj, w_out, norm_w, ln)


def _swiglu_tile(xb, wg_ref, wu_ref, wd_ref, tf):
    acc = jnp.zeros((xb.shape[0], wd_ref.shape[1]), F32)
    for f0 in range(0, wg_ref.shape[1], tf):
        g = jnp.dot(xb, wg_ref[:, f0:f0 + tf], preferred_element_type=F32)
        u = jnp.dot(xb, wu_ref[:, f0:f0 + tf], preferred_element_type=F32)
        acc = acc + jnp.dot((_silu(g) * u).astype(BF16), wd_ref[f0:f0 + tf, :], preferred_element_type=F32)
    return acc


def _ffn_kernel(x_ref, wg_ref, wu_ref, wd_ref, ln_ref, o_ref, *, tf):
    x = x_ref[...]
    z = DN_ALPHA * x + _swiglu_tile(x.astype(BF16), wg_ref, wu_ref, wd_ref, tf)
    o_ref[...] = _layernorm_rows(z, ln_ref[0:1, :], ln_ref[1:2, :])


def _ffn(x, wg, wu, wd, ln):
    T, D = x.shape
    F = wg.shape[1]
    tm = TM_FFN
    kern = functools.partial(_ffn_kernel, tf=TF_FFN)
    resident = dict(pipeline_mode=pl.Buffered(1))
    return pl.pallas_call(
        kern,
        grid=(T // tm,),
        in_specs=[pl.BlockSpec((tm, D), lambda i: (i, 0)),
                  pl.BlockSpec((D, F), lambda i: (0, 0), **resident),
                  pl.BlockSpec((D, F), lambda i: (0, 0), **resident),
                  pl.BlockSpec((F, D), lambda i: (0, 0), **resident),
                  pl.BlockSpec((SUBLANE, D), lambda i: (0, 0))],
        out_specs=pl.BlockSpec((tm, D), lambda i: (i, 0)),
        out_shape=jax.ShapeDtypeStruct((T, D), F32),
        compiler_params=pltpu.CompilerParams(dimension_semantics=("parallel",),
                                             vmem_limit_bytes=VMEM_LIMIT),
        name="ffn",
    )(x, wg, wu, wd, ln)


I_E1, I_E2, I_R1, I_R2, I_G1, I_G2 = range(6)


def _route_kernel(x_ref, r_ref, info_ref, cnt_ref, run_ref, *, ne):
    @pl.when(pl.program_id(0) == 0)
    def _():
        run_ref[...] = jnp.zeros(run_ref.shape, F32)

    logits = jnp.dot(x_ref[...], r_ref[...], precision=HIGHEST, preferred_element_type=F32)
    tm = logits.shape[0]
    lane = lax.broadcasted_iota(jnp.int32, logits.shape, 1)
    lanef = lane.astype(F32)
    logits = jnp.where(lane < ne, logits, NEG)
    m1 = jnp.max(logits, -1, keepdims=True)
    i1 = jnp.min(jnp.where(logits == m1, lanef, float(LANE)), -1, keepdims=True)
    oh1 = lanef == i1
    rest = jnp.where(oh1, NEG, logits)
    m2 = jnp.max(rest, -1, keepdims=True)
    i2 = jnp.min(jnp.where(rest == m2, lanef, float(LANE)), -1, keepdims=True)
    oh2 = lanef == i2
    e2 = jnp.exp(m2 - m1)
    g1 = 1.0 / (1.0 + e2)
    g2 = e2 / (1.0 + e2)
    occ = jnp.where(oh1, 1.0, 0.0) + jnp.where(oh2, 1.0, 0.0)
    r = lax.broadcasted_iota(jnp.int32, (tm, tm), 0)
    c = lax.broadcasted_iota(jnp.int32, (tm, tm), 1)
    earlier = jnp.where(c < r, 1.0, 0.0).astype(BF16)
    base = run_ref[...] + jnp.dot(earlier, occ.astype(BF16), preferred_element_type=F32)
    rank1 = jnp.sum(jnp.where(oh1, base, 0.0), -1, keepdims=True)
    rank2 = jnp.sum(jnp.where(oh2, base, 0.0), -1, keepdims=True)
    total = run_ref[...] + jnp.sum(occ, axis=0, keepdims=True)
    run_ref[...] = total
    cnt_ref[...] = jnp.broadcast_to(total, cnt_ref.shape)
    info = jnp.zeros(logits.shape, F32)
    for idx, val in ((I_E1, i1), (I_E2, i2), (I_R1, rank1), (I_R2, rank2), (I_G1, g1), (I_G2, g2)):
        info = jnp.where(lane == idx, val, info)
    info_ref[...] = info


def _route(x, router):
    T, D = x.shape
    tm = TM_ROUTE
    kern = functools.partial(_route_kernel, ne=N_EXPERTS)
    return pl.pallas_call(
        kern,
        grid=(T // tm,),
        in_specs=[pl.BlockSpec((tm, D), lambda i: (i, 0)),
                  pl.BlockSpec((D, LANE), lambda i: (0, 0))],
        out_specs=[pl.BlockSpec((tm, LANE), lambda i: (i, 0)),
                   pl.BlockSpec((SUBLANE, LANE), lambda i: (0, 0))],
        out_shape=[jax.ShapeDtypeStruct((T, LANE), F32),
                   jax.ShapeDtypeStruct((SUBLANE, LANE), F32)],
        scratch_shapes=[pltpu.VMEM((1, LANE), F32)],
        compiler_params=pltpu.CompilerParams(dimension_semantics=("arbitrary",),
                                             vmem_limit_bytes=VMEM_LIMIT),
        name="route",
    )(x, router)


def _row_copy(src_ref, src_row, dst_ref, dst_row, sem):
    return pltpu.make_async_copy(src_ref.at[pl.ds(src_row, 1), :], dst_ref.at[pl.ds(dst_row, 1), :], sem)


def _for_rows(tm, fn):
    def body(g, carry):
        for k in range(ROW_UNROLL):
            fn(g * ROW_UNROLL + k, k)
        return carry

    lax.fori_loop(0, tm // ROW_UNROLL, body, 0)


def _dispatch_kernel(p1_ref, p2_ref, x_ref, xs_in_ref, xs_ref, sem, *, tm):
    del xs_in_ref

    def copies(r):
        return (_row_copy(x_ref, r, xs_ref, p1_ref[0, 0, r], sem),
                _row_copy(x_ref, r, xs_ref, p2_ref[0, 0, r], sem))

    def start(r, k):
        for j, cp in enumerate(copies(r)):
            cp.start(priority=(k + j) % 2)

    def wait(r, k):
        for cp in copies(r):
            cp.wait()

    _for_rows(tm, start)
    _for_rows(tm, wait)


def _dispatch(x, pos1, pos2, n_rows):
    T, D = x.shape
    tm = TM_ROUTE
    kern = functools.partial(_dispatch_kernel, tm=tm)
    pos_spec = pl.BlockSpec((1, 1, tm), lambda i: (i, 0, 0), memory_space=pltpu.SMEM)
    return pl.pallas_call(
        kern,
        grid=(T // tm,),
        in_specs=[pos_spec, pos_spec,
                  pl.BlockSpec((tm, D), lambda i: (i, 0)),
                  pl.BlockSpec(memory_space=pl.ANY)],
        out_specs=pl.BlockSpec(memory_space=pl.ANY),
        out_shape=jax.ShapeDtypeStruct((n_rows, D), F32),
        scratch_shapes=[pltpu.SemaphoreType.DMA(())],
        input_output_aliases={3: 0},
        compiler_params=pltpu.CompilerParams(dimension_semantics=("arbitrary",),
                                             vmem_limit_bytes=VMEM_LIMIT),
        name="dispatch",
    )(pos1, pos2, x, jnp.zeros((n_rows, D), F32))


def _gffn_kernel(te_ref, tv_ref, x_ref, wg_ref, wu_ref, wd_ref, y_ref, *, tf):
    del te_ref
    valid = tv_ref[pl.program_id(0)] == 1

    @pl.when(valid)
    def _():
        y_ref[...] = _swiglu_tile(x_ref[...].astype(BF16), wg_ref, wu_ref, wd_ref, tf)

    @pl.when(jnp.logical_not(valid))
    def _():
        y_ref[...] = jnp.zeros(y_ref.shape, F32)


def _gffn(xs, tile_expert, tile_valid, wg, wu, wd):
    P, D = xs.shape
    _, _, F = wg.shape
    tm = TM_GFFN
    kern = functools.partial(_gffn_kernel, tf=TF_MOE)
    grid_spec = pltpu.PrefetchScalarGridSpec(
        num_scalar_prefetch=2,
        grid=(P // tm,),
        in_specs=[pl.BlockSpec((tm, D), lambda i, te, tv: (i, 0)),
                  pl.BlockSpec((None, D, F), lambda i, te, tv: (te[i], 0, 0), pipeline_mode=pl.Buffered(1)),
                  pl.BlockSpec((None, D, F), lambda i, te, tv: (te[i], 0, 0), pipeline_mode=pl.Buffered(1)),
                  pl.BlockSpec((None, F, D), lambda i, te, tv: (te[i], 0, 0), pipeline_mode=pl.Buffered(1))],
        out_specs=pl.BlockSpec((tm, D), lambda i, te, tv: (i, 0)),
    )
    return pl.pallas_call(
        kern,
        grid_spec=grid_spec,
        out_shape=jax.ShapeDtypeStruct((P, D), F32),
        compiler_params=pltpu.CompilerParams(dimension_semantics=("arbitrary",),
                                             vmem_limit_bytes=VMEM_LIMIT),
        name="gffn",
    )(tile_expert, tile_valid, xs, wg, wu, wd)


def _combine_kernel(p1_ref, p2_ref, p1n_ref, p2n_ref, x_ref, info_ref, y_ref, ln_ref, o_ref,
                    ya_ref, yb_ref, sem, *, tm, nt):
    i = pl.program_id(0)
    slot = i % 2

    def copies(pa_ref, pb_ref, r, s):
        return (_row_copy(y_ref, pa_ref[0, 0, r], ya_ref.at[s], r, sem.at[s]),
                _row_copy(y_ref, pb_ref[0, 0, r], yb_ref.at[s], r, sem.at[s]))

    def start_tile(pa_ref, pb_ref, s):
        def start(r, k):
            for j, cp in enumerate(copies(pa_ref, pb_ref, r, s)):
                cp.start(priority=(k + j) % 2)
        _for_rows(tm, start)

    @pl.when(i == 0)
    def _():
        start_tile(p1_ref, p2_ref, 0)

    @pl.when(i + 1 < nt)
    def _():
        start_tile(p1n_ref, p2n_ref, 1 - slot)

    def wait(r, k):
        for cp in copies(p1_ref, p2_ref, r, slot):
            cp.wait()

    _for_rows(tm, wait)
    info = info_ref[...]
    g1 = info[:, I_G1:I_G1 + 1]
    g2 = info[:, I_G2:I_G2 + 1]
    z = DN_ALPHA * x_ref[...] + (g1 * ya_ref[slot] + g2 * yb_ref[slot])
    o_ref[...] = _layernorm_rows(z, ln_ref[0:1, :], ln_ref[1:2, :])


def _combine(x, info, y, pos1, pos2, ln, row0, n_rows):
    _, D = x.shape
    tm = TM_ROUTE
    assert row0 % tm == 0 and n_rows % tm == 0
    t0 = row0 // tm
    nt = n_rows // tm
    kern = functools.partial(_combine_kernel, tm=tm, nt=nt)
    pos_spec = pl.BlockSpec((1, 1, tm), lambda i: (t0 + i, 0, 0), memory_space=pltpu.SMEM)
    next_spec = pl.BlockSpec((1, 1, tm), lambda i: (t0 + jnp.minimum(i + 1, nt - 1), 0, 0),
                             memory_space=pltpu.SMEM)
    return pl.pallas_call(
        kern,
        grid=(nt,),
        in_specs=[pos_spec, pos_spec, next_spec, next_spec,
                  pl.BlockSpec((tm, D), lambda i: (t0 + i, 0)),
                  pl.BlockSpec((tm, LANE), lambda i: (t0 + i, 0)),
                  pl.BlockSpec(memory_space=pl.ANY),
                  pl.BlockSpec((SUBLANE, D), lambda i: (0, 0))],
        out_specs=pl.BlockSpec((tm, D), lambda i: (i, 0)),
        out_shape=jax.ShapeDtypeStruct((n_rows, D), F32),
        scratch_shapes=[pltpu.VMEM((2, tm, D), F32), pltpu.VMEM((2, tm, D), F32),
                        pltpu.SemaphoreType.DMA((2,))],
        compiler_params=pltpu.CompilerParams(dimension_semantics=("arbitrary",),
                                             vmem_limit_bytes=VMEM_LIMIT),
        name="combine",
    )(pos1, pos2, pos1, pos2, x, info, y, ln)


def _moe(x, router, wg, wu, wd, ln, row_ranges):
    T, D = x.shape
    ne = wg.shape[0]
    tm = TM_GFFN
    n_tiles = (N_TOP * T) // tm + ne
    info, cnt = _route(x, router)
    counts = cnt[0, :ne].astype(jnp.int32)
    tiles_e = (counts + tm - 1) // tm
    tile_end = jnp.cumsum(tiles_e)
    offs = (tile_end - tiles_e) * tm
    tid = jnp.arange(n_tiles, dtype=jnp.int32)
    tile_expert = jnp.minimum(jnp.sum(tid[:, None] >= tile_end[None, :], axis=1), ne - 1).astype(jnp.int32)
    tile_valid = (tid < tile_end[-1]).astype(jnp.int32)
    eids = jnp.arange(ne, dtype=jnp.int32)[None, :]

    def slot(e_lane, r_lane):
        e = info[:, e_lane].astype(jnp.int32)
        off = jnp.sum(jnp.where(e[:, None] == eids, offs[None, :], 0), axis=1)
        return (off + info[:, r_lane].astype(jnp.int32)).reshape(T // TM_ROUTE, 1, TM_ROUTE)

    pos1 = slot(I_E1, I_R1)
    pos2 = slot(I_E2, I_R2)
    xs = _dispatch(x, pos1, pos2, n_tiles * tm)
    y = _gffn(xs, tile_expert, tile_valid, wg, wu, wd)
    return [_combine(x, info, y, pos1, pos2, ln, row0, n_rows) for row0, n_rows in row_ranges]


def _pad_rows(a, rows):
    return jnp.concatenate([a, jnp.zeros((rows - a.shape[0],) + a.shape[1:], a.dtype)], axis=0)


def _pack_w_in(w):
    gq = GDN_HEADS * GDN_DK
    gv = GDN_HEADS * GDN_DV
    sw = SSD_HEADS * SSD_HEADDIM
    sbc = SSD_GROUPS * SSD_STATE
    mw = MLSTM_HEADS * MLSTM_DH
    sizes = (gq, gq, gv, gv, N_DIR * GDN_HEADS, N_DIR * GDN_HEADS,
             sw, sw, sbc, sbc, N_DIR * SSD_HEADS,
             mw, mw, mw, mw, N_DIR * MLSTM_HEADS, N_DIR * MLSTM_HEADS)
    idx = np.cumsum(sizes)[:-1].tolist()
    (a_q, a_k, a_v, a_z, a_b, a_a, s_z, s_x, s_b, s_c, s_dt,
     c_q, c_k, c_v, c_o, c_i, c_f) = jnp.split(w, idx, axis=1)
    gates = jnp.concatenate([a_b, a_a, s_dt, s_dt, c_i, c_f], axis=1)
    gates = jnp.concatenate([gates, jnp.zeros((w.shape[0], LANE - gates.shape[1]), w.dtype)], axis=1)
    packed = jnp.concatenate([a_q, a_k, a_v, s_x, s_b, s_c, gates, c_q, c_k, c_v, a_z, s_z, c_o], axis=1)
    return packed.astype(BF16)


def _pack_gate_params(a_A_log, a_dt_bias, b_A_log, b_dt_bias, c_i_bias, c_f_bias):
    z8 = jnp.zeros((N_DIR * GDN_HEADS,), F32)
    z16 = jnp.zeros((N_DIR * SSD_HEADS,), F32)
    bias = jnp.concatenate([z8, a_dt_bias.reshape(-1), b_dt_bias.reshape(-1), b_dt_bias.reshape(-1),
                            c_i_bias.reshape(-1), c_f_bias.reshape(-1)])
    coef = jnp.concatenate([z8, -jnp.exp(a_A_log.reshape(-1)), z16, -jnp.exp(b_A_log.reshape(-1)), z8, z8])
    rows = jnp.stack([bias, coef]).astype(F32)
    rows = jnp.concatenate([rows, jnp.zeros((2, LANE - rows.shape[1]), F32)], axis=1)
    return _pad_rows(rows, SUBLANE)


def _mixer_layer(xs, ns, seq, w_in, conv_a_w, a_A_log, a_dt_bias, a_norm_w, conv_b_w, conv_b_b,
                 b_A_log, b_dt_bias, b_D, b_norm_w, c_i_bias, c_f_bias, c_norm_w, w_out, ln_g, ln_b):
    proj, proj_h = _in_proj(xs, _pack_w_in(w_in))
    conv_w = _pad_rows(jnp.concatenate([conv_a_w, conv_b_w], axis=1).astype(F32), SUBLANE)
    conv_b = jnp.concatenate([jnp.zeros((conv_a_w.shape[1],), F32), conv_b_b.astype(F32)])[None, :]
    gate_p = _pack_gate_params(a_A_log, a_dt_bias, b_A_log, b_dt_bias, c_i_bias, c_f_bias)
    act, feat, ftc = _prep(proj, conv_w, conv_b, gate_p, ns, seq)
    og = _gdn(act, feat, ftc, ns, seq)
    d_row = jnp.repeat(b_D.astype(F32), SSD_HEADDIM)[None, :]
    ys = _ssd(act, feat, ftc, d_row, ns, seq)
    hm = _mlstm(proj_h, feat, ftc, ns, seq)
    norm_w = _pad_rows(jnp.stack([jnp.tile(a_norm_w, GDN_HEADS), b_norm_w, c_norm_w]).astype(F32), SUBLANE)
    ln = _pad_rows(jnp.stack([ln_g, ln_b]).astype(F32), SUBLANE)
    return _post(og, ys, hm, proj_h, xs, w_out.astype(BF16), norm_w, ln)


def kernel(x_prompt, x_sample, w_in, conv_a_w, a_A_log, a_dt_bias, a_norm_w, conv_b_w, conv_b_b, b_A_log, b_dt_bias, b_D, b_norm_w, c_i_bias, c_f_bias, c_norm_w, w_out, ln1_g, ln1_b, ln2_g, ln2_b, ffn_w_gate, ffn_w_up, ffn_w_down, moe_router, moe_w_gate, moe_w_up, moe_w_down):
    bp, seq, dm = x_prompt.shape
    bs, seq_s, _ = x_sample.shape
    assert seq == seq_s and dm == D_MODEL and w_in.shape[0] == DEPTH
    assert seq % TB_MIX == 0 and seq % TB_PREP == 0
    ns = bp + bs
    xs = [x_prompt.reshape(bp * seq, dm), x_sample.reshape(bs * seq, dm)]
    for l in range(DEPTH):
        x = _mixer_layer(xs if l == 0 else [x], ns, seq, w_in[l], conv_a_w[l], a_A_log[l], a_dt_bias[l], a_norm_w[l],
                         conv_b_w[l], conv_b_b[l], b_A_log[l], b_dt_bias[l], b_D[l], b_norm_w[l],
                         c_i_bias[l], c_f_bias[l], c_norm_w[l], w_out[l], ln1_g[l], ln1_b[l])
        ln2 = _pad_rows(jnp.stack([ln2_g[l], ln2_b[l]]).astype(F32), SUBLANE)
        j = l // 2
        if l % 2 == 0:
            x = _ffn(x, ffn_w_gate[j].astype(BF16), ffn_w_up[j].astype(BF16), ffn_w_down[j].astype(BF16), ln2)
        else:
            router = jnp.concatenate(
                [moe_router[j], jnp.zeros((dm, LANE - N_EXPERTS), moe_router.dtype)], axis=1).astype(F32)
            ranges = [(0, bp * seq), (bp * seq, bs * seq)] if l == DEPTH - 1 else [(0, ns * seq)]
            outs = _moe(x, router, moe_w_gate[j].astype(BF16), moe_w_up[j].astype(BF16),
                        moe_w_down[j].astype(BF16), ln2, ranges)
            if l == DEPTH - 1:
                return (outs[0].reshape(bp, seq, dm), outs[1].reshape(bs, seq, dm))
            x = outs[0]
    y_prompt = x[:bp * seq].reshape(bp, seq, dm)
    y_sample = x[bp * seq:].reshape(bs, seq, dm)
    return (y_prompt, y_sample)
```

```python
import functools

import jax
import jax.numpy as jnp
import numpy as np
from jax import lax
from jax.experimental import pallas as pl
from jax.experimental.pallas import tpu as pltpu

F32 = jnp.float32
BF16 = jnp.bfloat16
HIGHEST = lax.Precision.HIGHEST

D_MODEL = 1024
DEPTH = 2
GDN_HEADS, GDN_DK, GDN_DV = 4, 128, 128
SSD_HEADS, SSD_HEADDIM, SSD_GROUPS, SSD_STATE = 8, 64, 2, 128
MLSTM_HEADS, MLSTM_DH = 4, 128
N_DIR = 2
CONV_W = 4
CONV_PAD_L = (CONV_W - 1) // 2
CHUNK = 64
N_EXPERTS = 8
DN_ALPHA = (2 * DEPTH) ** 0.25
LN_EPS = 1e-5
NORM_EPS = 1e-6
NEG = -1e30

LANE = 128
SUBLANE = 8

HW = 512
COL_AQKV = 0
COL_SXBC = 3 * HW
N_CONV = 5 * HW
COL_GATE = N_CONV
N_PROJ_F32 = COL_GATE + LANE
COL_CQKV = 0
COL_Z = 3 * HW
N_PROJ_BF16 = 6 * HW

F_BETA, F_G, F_DT, F_ACS, F_IPRE, F_BCUM = 0, 8, 16, 32, 48, 56

TM_PROJ = 512
TB_PREP = 256
TB_MIX = 512
TB_GDN = 2048
TB_POST = 512
TM_FFN = 512
TF_FFN = 256
TM_ROUTE = 512
TM_GFFN = 512
TF_MOE = 512
N_TOP = 2
ROW_UNROLL = 8
GDN_GROUPS = 8
VMEM_LIMIT = 56 * 1024 * 1024


def _sigmoid(x):
    return 1.0 / (1.0 + jnp.exp(-x))


def _silu(x):
    return x * _sigmoid(x)


def _softplus(x):
    return jnp.maximum(x, 0.0) + jnp.log(1.0 + jnp.exp(-jnp.abs(x)))


def _mm(a, b):
    return jnp.dot(a.astype(BF16), b.astype(BF16), preferred_element_type=F32)


def _mm_nt(a, b):
    return lax.dot_general(a.astype(BF16), b.astype(BF16), (((1,), (1,)), ((), ())),
                           preferred_element_type=F32)


def _mm_tn(a, b):
    return lax.dot_general(a.astype(BF16), b.astype(BF16), (((0,), (0,)), ((), ())),
                           preferred_element_type=F32)


def _layernorm_rows(z, g, b):
    mu = jnp.mean(z, -1, keepdims=True)
    zc = z - mu
    var = jnp.mean(zc * zc, -1, keepdims=True)
    return zc * lax.rsqrt(var + LN_EPS) * g + b


def _row_sources(xs, tm):
    specs, starts, t = [], [], 0
    for x in xs:
        n = x.shape[0] // tm
        assert n * tm == x.shape[0]
        specs.append(pl.BlockSpec((tm, x.shape[1]), lambda i, t=t, n=n: (jnp.clip(i - t, 0, n - 1), 0)))
        starts.append(t)
        t += n
    return specs, tuple(starts), t


def _pick_tile(i, refs, starts):
    x = refs[0][...]
    for r, s in zip(refs[1:], starts[1:]):
        x = jnp.where(i >= s, r[...], x)
    return x


def _inproj_kernel(*refs, starts):
    x_refs = refs[:len(starts)]
    w_ref, of_ref, oh_ref = refs[len(starts):]
    xb = _pick_tile(pl.program_id(0), x_refs, starts).astype(BF16)
    of_ref[...] = jnp.dot(xb, w_ref[:, 0:N_PROJ_F32], preferred_element_type=F32)
    oh_ref[...] = jnp.dot(xb, w_ref[:, N_PROJ_F32:N_PROJ_F32 + N_PROJ_BF16],
                          preferred_element_type=F32).astype(BF16)


def _in_proj(xs, w):
    D, N = w.shape
    assert N == N_PROJ_F32 + N_PROJ_BF16
    x_specs, starts, nt = _row_sources(xs, TM_PROJ)
    T = nt * TM_PROJ
    return pl.pallas_call(
        functools.partial(_inproj_kernel, starts=starts),
        grid=(nt,),
        in_specs=x_specs + [pl.BlockSpec((D, N), lambda i: (0, 0), pipeline_mode=pl.Buffered(1))],
        out_specs=[pl.BlockSpec((TM_PROJ, N_PROJ_F32), lambda i: (i, 0)),
                   pl.BlockSpec((TM_PROJ, N_PROJ_BF16), lambda i: (i, 0))],
        out_shape=[jax.ShapeDtypeStruct((T, N_PROJ_F32), F32),
                   jax.ShapeDtypeStruct((T, N_PROJ_BF16), BF16)],
        compiler_params=pltpu.CompilerParams(dimension_semantics=("parallel",),
                                             vmem_limit_bytes=VMEM_LIMIT),
        name="in_proj",
    )(*xs, w)


def _prep_kernel(xm_ref, xp_ref, xn_ref, g_ref, cw_ref, cb_ref, gp_ref,
                 act_ref, f_ref, ftc_ref, pad_ref, *, tb, nb):
    b = pl.program_id(1)
    has_prev = (b > 0).astype(F32)
    has_next = (b < nb - 1).astype(F32)
    pad_ref[0:SUBLANE, :] = xp_ref[...] * has_prev
    pad_ref[SUBLANE:SUBLANE + tb, :] = xm_ref[...]
    pad_ref[SUBLANE + tb:2 * SUBLANE + tb, :] = xn_ref[...] * has_next

    base = SUBLANE - CONV_PAD_L
    for c0 in range(0, N_CONV, HW):
        y = cb_ref[:, c0:c0 + HW]
        for j in range(CONV_W):
            y = y + pad_ref[base + j:base + j + tb, c0:c0 + HW] * cw_ref[j:j + 1, c0:c0 + HW]
        y = _silu(y)
        if c0 in (COL_AQKV, COL_AQKV + HW):
            scale = GDN_DK ** -0.5 if c0 == COL_AQKV else 1.0
            parts = []
            for h in range(GDN_HEADS):
                seg = y[:, h * GDN_DK:(h + 1) * GDN_DK]
                ss = jnp.sum(seg * seg, -1, keepdims=True)
                parts.append(seg * lax.rsqrt(ss + NORM_EPS) * scale)
            y = jnp.concatenate(parts, axis=1)
        act_ref[:, c0:c0 + HW] = y.astype(BF16)

    p = g_ref[...]
    z = p + gp_ref[0:1, :]
    coef = gp_ref[1:2, :]
    lane = lax.broadcasted_iota(jnp.int32, p.shape, 1)
    sp = _softplus(z)
    val = jnp.where(lane < F_G, _sigmoid(p),
          jnp.where(lane < F_DT, coef * sp,
          jnp.where(lane < F_ACS, sp,
          jnp.where(lane < F_IPRE, coef * sp,
          jnp.where(lane < F_BCUM, z, -_softplus(-z))))))
    is_cum = ((lane >= F_G) & (lane < F_DT)) | ((lane >= F_ACS) & (lane < F_IPRE)) | (
        (lane >= F_BCUM) & (lane < F_BCUM + 8))
    wide = (lane >= F_DT) & (lane < F_IPRE)
    col_bwd = jnp.where(wide, lane % 16, 2 * (lane % 8)) >= 8
    r = lax.broadcasted_iota(jnp.int32, (tb, tb), 0)
    c = lax.broadcasted_iota(jnp.int32, (tb, tb), 1)
    same = (r // CHUNK) == (c // CHUNK)
    tril = jnp.where(same & (c <= r), 1.0, 0.0).astype(F32)
    triu = jnp.where(same & (c >= r), 1.0, 0.0).astype(F32)
    cs_f = jnp.dot(tril, val, precision=HIGHEST, preferred_element_type=F32)
    cs_b = jnp.dot(triu, val, precision=HIGHEST, preferred_element_type=F32)
    feat = jnp.where(is_cum, jnp.where(col_bwd, cs_b, cs_f), val)
    f_ref[...] = feat
    for k in range(tb // CHUNK):
        ftc_ref[k] = feat[k * CHUNK:(k + 1) * CHUNK, :].T


def _prep(proj, conv_w, conv_b, gate_p, ns, seq):
    T = proj.shape[0]
    tb = TB_PREP
    nb = seq // tb
    t8 = tb // SUBLANE
    n8 = T // SUBLANE
    kern = functools.partial(_prep_kernel, tb=tb, nb=nb)
    return pl.pallas_call(
        kern,
        grid=(ns, nb),
        in_specs=[
            pl.BlockSpec((tb, N_CONV), lambda s, b: (s * nb + b, 0)),
            pl.BlockSpec((SUBLANE, N_CONV), lambda s, b: (jnp.maximum((s * nb + b) * t8 - 1, 0), 0)),
            pl.BlockSpec((SUBLANE, N_CONV), lambda s, b: (jnp.minimum((s * nb + b + 1) * t8, n8 - 1), 0)),
            pl.BlockSpec((tb, LANE), lambda s, b: (s * nb + b, COL_GATE // LANE)),
            pl.BlockSpec((SUBLANE, N_CONV), lambda s, b: (0, 0)),
            pl.BlockSpec((1, N_CONV), lambda s, b: (0, 0)),
            pl.BlockSpec((SUBLANE, LANE), lambda s, b: (0, 0)),
        ],
        out_specs=[
            pl.BlockSpec((tb, N_CONV), lambda s, b: (s * nb + b, 0)),
            pl.BlockSpec((tb, LANE), lambda s, b: (s * nb + b, 0)),
            pl.BlockSpec((tb // CHUNK, LANE, CHUNK), lambda s, b: (s * nb + b, 0, 0)),
        ],
        out_shape=[
            jax.ShapeDtypeStruct((T, N_CONV), BF16),
            jax.ShapeDtypeStruct((T, LANE), F32),
            jax.ShapeDtypeStruct((T // CHUNK, LANE, CHUNK), F32),
        ],
        scratch_shapes=[pltpu.VMEM((tb + 2 * SUBLANE, N_CONV), F32)],
        compiler_params=pltpu.CompilerParams(dimension_semantics=("parallel", "parallel"),
                                             vmem_limit_bytes=VMEM_LIMIT),
        name="prep",
    )(proj, proj, proj, proj, conv_w, conv_b, gate_p)


def _expand_features(f_ref, e_ref, targets, d):
    n = e_ref.shape[1]
    crow = lax.broadcasted_iota(jnp.int32, (LANE, n), 0)
    lcol = lax.broadcasted_iota(jnp.int32, (LANE, n), 1)
    tgt = jnp.zeros((LANE, n), jnp.int32)
    off = 0
    for width, base, heads, per in targets:
        seg = base + d * heads + (lcol - off) // per
        tgt = jnp.where((lcol >= off) & (lcol < off + width), seg, tgt)
        off += width
    sel = jnp.where(crow == tgt, 1.0, 0.0).astype(BF16)
    f = f_ref[...]
    hi = f.astype(BF16)
    r1 = f - hi.astype(F32)
    mid = r1.astype(BF16)
    lo = (r1 - mid.astype(F32)).astype(BF16)
    e_ref[...] = (jnp.dot(jnp.concatenate([hi, mid], axis=1), jnp.concatenate([sel, sel], axis=0),
                          preferred_element_type=F32)
                  + jnp.dot(lo, sel, preferred_element_type=F32))


def _dir_masks(d):
    row = lax.broadcasted_iota(jnp.int32, (CHUNK, CHUNK), 0)
    col = lax.broadcasted_iota(jnp.int32, (CHUNK, CHUNK), 1)
    diff = (row - col) * (1 - 2 * d)
    return row, col, diff >= 0, diff > 0


def _mix_row_map(nb):
    return lambda s, d, i: s * nb + i + d * (nb - 1 - 2 * i)


def _tri_inverse_all(a_list, eye, m8, m16, m32, m64):
    a8 = [jnp.where(m8, a, 0.0) for a in a_list]
    x = [eye - a for a in a8]
    a8 = [a.astype(BF16) for a in a8]
    p2 = [_mm(a, a).astype(BF16) for a in a8]
    yield
    x = [xi + _mm(xi, pi) for xi, pi in zip(x, p2)]
    p4 = [_mm(pi, pi) for pi in p2]
    yield
    x = [xi + _mm(xi, pi) for xi, pi in zip(x, p4)]
    yield
    for m in (m16, m32, m64):
        xb = [t.astype(BF16) for t in x]
        y = [_mm(jnp.where(m, a, 0.0), xi) for a, xi in zip(a_list, xb)]
        yield
        x = [xf - _mm(xi, yi) for xf, xi, yi in zip(x, xb, y)]
        yield
    return x


def _drain(*gens):
    live = list(gens)
    while live:
        for g in list(live):
            try:
                next(g)
            except StopIteration:
                live.remove(g)


def _gdn_kernel(q_ref, k_ref, v_ref, f_ref, ftc_ref, o_ref, s_ref, e_ref, *, tb):
    d = pl.program_id(1)
    i = pl.program_id(2)

    @pl.when(i == 0)
    def _():
        s_ref[...] = jnp.zeros(s_ref.shape, F32)

    ncb = tb // CHUNK
    _expand_features(f_ref, e_ref, [(HW, F_BETA, GDN_HEADS, GDN_DV), (HW, F_G, GDN_HEADS, GDN_DV)], d)
    row, col, incl, strict = _dir_masks(d)
    eye = jnp.where(row == col, 1.0, 0.0).astype(F32)
    b8 = (row // 8) == (col // 8)
    b16 = (row // 16) == (col // 16)
    b32 = (row // 32) == (col // 32)
    m16 = b16 & jnp.logical_not(b8)
    m32 = b32 & jnp.logical_not(b16)
    m64 = jnp.logical_not(b32)
    tend = (CHUNK - 1) * (1 - d)

    heads = range(GDN_HEADS)

    def prepare(chunks, out):
        rows, ges = [], []
        qb, kb, rhs, qde, kde, bcol, dec = [], [], [], [], [], [], []
        for cc in chunks:
            c = cc + d * (ncb - 1 - 2 * cc)
            r0 = pl.multiple_of(c * CHUNK, CHUNK)
            rows.append(r0)
            bx = e_ref[pl.ds(r0, CHUNK), 0:HW]
            gx = e_ref[pl.ds(r0, CHUNK), HW:2 * HW]
            gend = e_ref[pl.ds(r0 + tend, 1), HW:2 * HW]
            eg = jnp.exp(gx)
            kdf = jnp.exp(gend - gx)
            ges.append(jnp.exp(gend))
            q16 = q_ref[pl.ds(r0, CHUNK), :]
            k16 = k_ref[pl.ds(r0, CHUNK), :]
            q = q16.astype(F32)
            k = k16.astype(F32)
            v = v_ref[pl.ds(r0, CHUNK), :].astype(F32)
            kbeta = k * bx
            vb = (v * bx).astype(BF16)
            kbe = (kbeta * eg).astype(BF16)
            qd = (q * eg).astype(BF16)
            kd = (k * kdf).astype(BF16)
            for h in heads:
                sl = slice(h * GDN_DV, (h + 1) * GDN_DV)
                qb.append(q16[:, sl])
                kb.append(k16[:, sl])
                rhs.append(jnp.concatenate([vb[:, sl], kbe[:, sl]], axis=1))
                qde.append(qd[:, sl])
                kde.append(kd[:, sl])
                bcol.append(bx[:, h * GDN_DV:h * GDN_DV + CHUNK])
                grow = ftc_ref[c, pl.ds(F_G + d * GDN_HEADS + h, 1), :]
                gd = gx[:, h * GDN_DV:h * GDN_DV + CHUNK] - grow
                dec.append(jnp.exp(jnp.where(incl, gd, NEG)))
        qkk = [_mm_nt(jnp.concatenate([qi, ki], axis=0), ki) for qi, ki in zip(qb, kb)]
        yield
        a = [jnp.where(strict, bi * t[CHUNK:2 * CHUNK] * di, 0.0) for bi, t, di in zip(bcol, qkk, dec)]
        tinv = yield from _tri_inverse_all(a, eye, b8, m16, m32, m64)
        uw = [_mm(ti, ri) for ti, ri in zip(tinv, rhs)]
        yield
        qkd = [(t[0:CHUNK] * di).astype(BF16) for t, di in zip(qkk, dec)]
        out.update(rows=rows, ges=ges, qde=qde, kde=kde, uw=uw, qkd=qkd)

    state = [s_ref[:, h * GDN_DV:(h + 1) * GDN_DV] for h in heads]

    def recur(pre):
        for n, r0 in enumerate(pre["rows"]):
            p0 = n * GDN_HEADS
            uw, qde, kde, qkd = pre["uw"], pre["qde"], pre["kde"], pre["qkd"]
            sb = [s.astype(BF16) for s in state]
            wq = [_mm(jnp.concatenate([uw[p0 + h][:, GDN_DV:2 * GDN_DV].astype(BF16), qde[p0 + h]], axis=0),
                      sb[h]) for h in heads]
            yield
            v_new = [(uw[p0 + h][:, 0:GDN_DV] - wq[h][0:CHUNK]).astype(BF16) for h in heads]
            outs = [wq[h][CHUNK:2 * CHUNK] + _mm(qkd[p0 + h], v_new[h]) for h in heads]
            for h in heads:
                state[h] = (state[h] * pre["ges"][n][:, h * GDN_DV:(h + 1) * GDN_DV]
                            + _mm_tn(kde[p0 + h], v_new[h]))
            yield
            o_ref[pl.ds(r0, CHUNK), :] = jnp.concatenate(outs, axis=1).astype(BF16)

    gsz = max(ncb // GDN_GROUPS, 1)
    pres = [{} for _ in range(0, ncb, gsz)]
    _drain(prepare(range(0, gsz), pres[0]))
    for n in range(1, len(pres)):
        _drain(prepare(range(n * gsz, (n + 1) * gsz), pres[n]), recur(pres[n - 1]))
    _drain(recur(pres[-1]))
    for h in heads:
        s_ref[:, h * GDN_DV:(h + 1) * GDN_DV] = state[h]


def _gdn(act, feat, ftc, ns, seq):
    T = act.shape[0]
    tb = TB_GDN
    nb = seq // tb
    rm = _mix_row_map(nb)
    kern = functools.partial(_gdn_kernel, tb=tb)
    qkv = [pl.BlockSpec((tb, HW), lambda s, d, i, j=j: (rm(s, d, i), j)) for j in range(3)]
    return pl.pallas_call(
        kern,
        grid=(ns, N_DIR, nb),
        in_specs=qkv + [
            pl.BlockSpec((tb, LANE), lambda s, d, i: (rm(s, d, i), 0)),
            pl.BlockSpec((tb // CHUNK, LANE, CHUNK), lambda s, d, i: (rm(s, d, i), 0, 0)),
        ],
        out_specs=pl.BlockSpec((None, tb, HW), lambda s, d, i: (d, rm(s, d, i), 0)),
        out_shape=jax.ShapeDtypeStruct((N_DIR, T, HW), BF16),
        scratch_shapes=[pltpu.VMEM((GDN_DK, HW), F32), pltpu.VMEM((tb, 2 * HW), F32)],
        compiler_params=pltpu.CompilerParams(
            dimension_semantics=("parallel", "parallel", "arbitrary"), vmem_limit_bytes=VMEM_LIMIT),
        name="gdn_scan",
    )(act, act, act, feat, ftc)


def _ssd_kernel(x_ref, b_ref, c_ref, f_ref, ftc_ref, dp_ref, o_ref, h_ref, e_ref, *, tb):
    d = pl.program_id(1)
    i = pl.program_id(2)

    @pl.when(i == 0)
    def _():
        h_ref[...] = jnp.zeros(h_ref.shape, F32)

    ncb = tb // CHUNK
    _expand_features(f_ref, e_ref, [(HW, F_DT, SSD_HEADS, SSD_HEADDIM), (HW, F_ACS, SSD_HEADS, SSD_HEADDIM)], d)
    tend = (CHUNK - 1) * (1 - d)
    row = lax.broadcasted_iota(jnp.int32, (CHUNK, LANE), 0)
    lane = lax.broadcasted_iota(jnp.int32, (CHUNK, LANE), 1)
    incl = (row - lane % CHUNK) * (1 - 2 * d) >= 0
    lo = lane < SSD_HEADDIM
    gw = HW // SSD_GROUPS
    spg = gw // LANE
    dfac = (1 - d).astype(F32)

    groups = range(SSD_GROUPS)
    rows, xs, eas, cds, cgs, bgs, xdtes, xhs, lms = [], [], [], [], [], [], [], [], []
    for cc in range(ncb):
        c = cc + d * (ncb - 1 - 2 * cc)
        r0 = pl.multiple_of(c * CHUNK, CHUNK)
        rows.append(r0)
        dtx = e_ref[pl.ds(r0, CHUNK), 0:HW]
        ax = e_ref[pl.ds(r0, CHUNK), HW:2 * HW]
        aend = e_ref[pl.ds(r0 + tend, 1), HW:2 * HW]
        x = x_ref[pl.ds(r0, CHUNK), :].astype(F32)
        xdt = x * dtx
        xdte = (xdt * jnp.exp(aend - ax)).astype(BF16)
        xs.append(x)
        eas.append(jnp.exp(ax))
        cds.append(jnp.exp(aend))
        bm = b_ref[pl.ds(r0, CHUNK), :]
        cm = c_ref[pl.ds(r0, CHUNK), :]
        for g in groups:
            bgs.append(bm[:, g * SSD_STATE:(g + 1) * SSD_STATE])
            cgs.append(cm[:, g * SSD_STATE:(g + 1) * SSD_STATE])
            xdtes.append(xdte[:, g * gw:(g + 1) * gw])
        for s in range(HW // LANE):
            xp = xdt[:, s * LANE:(s + 1) * LANE].astype(BF16)
            zero = jnp.zeros_like(xp)
            xhs.append(jnp.concatenate([jnp.where(lo, xp, zero), jnp.where(lo, zero, xp)], axis=0))
            arow = jnp.concatenate(
                [ftc_ref[c, pl.ds(F_ACS + d * SSD_HEADS + 2 * s, 1), :],
                 ftc_ref[c, pl.ds(F_ACS + d * SSD_HEADS + 2 * s + 1, 1), :]], axis=1)
            lms.append(jnp.exp(jnp.where(incl, ax[:, s * LANE:(s + 1) * LANE] - arow, NEG)))
    cbs = [_mm_nt(cg, jnp.concatenate([bg, bg], axis=0)) for cg, bg in zip(cgs, bgs)]
    sts = [_mm_tn(bg, xe) for bg, xe in zip(bgs, xdtes)]
    nslab = HW // LANE
    yds = [_mm(cbs[(j // nslab) * SSD_GROUPS + (j % nslab) // spg] * lms[j], xhs[j])
           for j in range(ncb * nslab)]

    state = [h_ref[:, g * gw:(g + 1) * gw] for g in groups]
    starts = []
    for cc in range(ncb):
        starts.append([s.astype(BF16) for s in state])
        state = [state[g] * cds[cc][:, g * gw:(g + 1) * gw] + sts[cc * SSD_GROUPS + g] for g in groups]
    for g in groups:
        h_ref[:, g * gw:(g + 1) * gw] = state[g]
    for cc in range(ncb):
        outs = []
        for g in groups:
            y_g = _mm(cgs[cc * SSD_GROUPS + g], starts[cc][g]) * eas[cc][:, g * gw:(g + 1) * gw]
            j0 = cc * nslab + g * spg
            outs.append(y_g + jnp.concatenate(yds[j0:j0 + spg], axis=1))
        y = jnp.concatenate(outs, axis=1) + xs[cc] * (dp_ref[...] * dfac)
        o_ref[pl.ds(rows[cc], CHUNK), :] = y.astype(BF16)


def _ssd(act, feat, ftc, d_row, ns, seq):
    T = act.shape[0]
    tb = TB_MIX
    nb = seq // tb
    rm = _mix_row_map(nb)
    kern = functools.partial(_ssd_kernel, tb=tb)
    bc_w = SSD_GROUPS * SSD_STATE
    return pl.pallas_call(
        kern,
        grid=(ns, N_DIR, nb),
        in_specs=[
            pl.BlockSpec((tb, HW), lambda s, d, i: (rm(s, d, i), COL_SXBC // HW)),
            pl.BlockSpec((tb, bc_w), lambda s, d, i: (rm(s, d, i), (COL_SXBC + HW) // bc_w)),
            pl.BlockSpec((tb, bc_w), lambda s, d, i: (rm(s, d, i), (COL_SXBC + HW) // bc_w + 1)),
            pl.BlockSpec((tb, LANE), lambda s, d, i: (rm(s, d, i), 0)),
            pl.BlockSpec((tb // CHUNK, LANE, CHUNK), lambda s, d, i: (rm(s, d, i), 0, 0)),
            pl.BlockSpec((1, HW), lambda s, d, i: (0, 0)),
        ],
        out_specs=pl.BlockSpec((None, tb, HW), lambda s, d, i: (d, rm(s, d, i), 0)),
        out_shape=jax.ShapeDtypeStruct((N_DIR, T, HW), BF16),
        scratch_shapes=[pltpu.VMEM((SSD_STATE, HW), F32), pltpu.VMEM((tb, 2 * HW), F32)],
        compiler_params=pltpu.CompilerParams(
            dimension_semantics=("parallel", "parallel", "arbitrary"), vmem_limit_bytes=VMEM_LIMIT),
        name="ssd_scan",
    )(act, act, act, feat, ftc, d_row)


def _mlstm_kernel(q_ref, k_ref, v_ref, f_ref, ftc_ref, o_ref, c_ref, m_ref, e_ref, *, tb):
    d = pl.program_id(1)
    i = pl.program_id(2)

    @pl.when(i == 0)
    def _():
        c_ref[...] = jnp.zeros(c_ref.shape, F32)
        m_ref[...] = jnp.zeros(m_ref.shape, F32)

    ncb = tb // CHUNK
    dh = MLSTM_DH
    _expand_features(f_ref, e_ref, [(HW, F_IPRE, MLSTM_HEADS, dh), (HW, F_BCUM, MLSTM_HEADS, dh)], d)
    _, _, incl, _ = _dir_masks(d)
    tend = (CHUNK - 1) * (1 - d)
    ones = jnp.ones((CHUNK, dh), BF16)

    heads = range(MLSTM_HEADS)
    rows, bxs, bends, mcs = [], [], [], []
    qb, kb, kw, vaug, dmat, mis = [], [], [], [], [], []
    for cc in range(ncb):
        c = cc + d * (ncb - 1 - 2 * cc)
        r0 = pl.multiple_of(c * CHUNK, CHUNK)
        rows.append(r0)
        ix = e_ref[pl.ds(r0, CHUNK), 0:HW]
        bx = e_ref[pl.ds(r0, CHUNK), HW:2 * HW]
        bend = e_ref[pl.ds(r0 + tend, 1), HW:2 * HW]
        logw = bend - bx + ix
        mc = jnp.max(logw, axis=0, keepdims=True)
        ws = jnp.exp(logw - mc)
        bxs.append(bx)
        bends.append(bend)
        mcs.append(mc)
        q = q_ref[pl.ds(r0, CHUNK), :]
        k = k_ref[pl.ds(r0, CHUNK), :]
        kws = (k.astype(F32) * (ws * (dh ** -0.5))).astype(BF16)
        v = v_ref[pl.ds(r0, CHUNK), :]
        for h in heads:
            sl = slice(h * dh, (h + 1) * dh)
            qb.append(q[:, sl])
            kb.append(k[:, sl])
            kw.append(kws[:, sl])
            vaug.append(jnp.concatenate([v[:, sl], ones], axis=1))
            brow = ftc_ref[c, pl.ds(F_BCUM + d * MLSTM_HEADS + h, 1), :]
            irow = ftc_ref[c, pl.ds(F_IPRE + d * MLSTM_HEADS + h, 1), :]
            logd = jnp.where(incl, bx[:, h * dh:h * dh + CHUNK] - brow + irow, NEG)
            mi = jnp.max(logd, axis=-1, keepdims=True)
            mis.append(mi)
            dmat.append(jnp.exp(logd - mi) * (dh ** -0.5))
    qk = [_mm_nt(qi, ki) for qi, ki in zip(qb, kb)]
    kv = [_mm_tn(ki, vi) for ki, vi in zip(kw, vaug)]
    intra = [_mm(qi * di, vi) for qi, di, vi in zip(qk, dmat, vaug)]

    state = [c_ref[:, h * 2 * dh:(h + 1) * 2 * dh] for h in heads]
    m_old = m_ref[...]
    starts, m_starts = [], []
    for cc in range(ncb):
        starts.append([s.astype(BF16) for s in state])
        m_starts.append(m_old)
        m_new = jnp.maximum(bends[cc] + m_old, mcs[cc])
        s_old = jnp.exp(bends[cc] + m_old - m_new)
        s_add = jnp.exp(mcs[cc] - m_new)
        new_state = []
        for h in heads:
            sl = slice(h * dh, (h + 1) * dh)
            so = jnp.concatenate([s_old[:, sl], s_old[:, sl]], axis=1)
            sa = jnp.concatenate([s_add[:, sl], s_add[:, sl]], axis=1)
            new_state.append(state[h] * so + kv[cc * MLSTM_HEADS + h] * sa)
        state = new_state
        m_old = m_new
    for h in heads:
        c_ref[:, h * 2 * dh:(h + 1) * 2 * dh] = state[h]
    m_ref[...] = m_old
    inter = [_mm(qb[cc * MLSTM_HEADS + h], starts[cc][h]) for cc in range(ncb) for h in heads]
    for cc in range(ncb):
        m_inter = bxs[cc] + m_starts[cc]
        outs = []
        for h in heads:
            j = cc * MLSTM_HEADS + h
            mint = m_inter[:, h * dh:(h + 1) * dh]
            m_t = jnp.maximum(mint, mis[j])
            s_inter = jnp.exp(mint - m_t)
            s_intra = jnp.exp(mis[j] - m_t)
            num = s_inter * inter[j][:, 0:dh] + s_intra * intra[j][:, 0:dh]
            den = s_inter * inter[j][:, dh:2 * dh] + s_intra * intra[j][:, dh:2 * dh]
            outs.append(num / jnp.maximum(jnp.abs(den), jnp.exp(-m_t)))
        o_ref[pl.ds(rows[cc], CHUNK), :] = jnp.concatenate(outs, axis=1).astype(BF16)


def _mlstm(proj, feat, ftc, ns, seq):
    T = proj.shape[0]
    tb = TB_MIX
    nb = seq // tb
    rm = _mix_row_map(nb)
    kern = functools.partial(_mlstm_kernel, tb=tb)
    qkv = [pl.BlockSpec((tb, HW), lambda s, d, i, j=j: (rm(s, d, i), COL_CQKV // HW + j)) for j in range(3)]
    return pl.pallas_call(
        kern,
        grid=(ns, N_DIR, nb),
        in_specs=qkv + [
            pl.BlockSpec((tb, LANE), lambda s, d, i: (rm(s, d, i), 0)),
            pl.BlockSpec((tb // CHUNK, LANE, CHUNK), lambda s, d, i: (rm(s, d, i), 0, 0)),
        ],
        out_specs=pl.BlockSpec((None, tb, HW), lambda s, d, i: (d, rm(s, d, i), 0)),
        out_shape=jax.ShapeDtypeStruct((N_DIR, T, HW), BF16),
        scratch_shapes=[pltpu.VMEM((MLSTM_DH, 2 * HW), F32), pltpu.VMEM((1, HW), F32),
                        pltpu.VMEM((tb, 2 * HW), F32)],
        compiler_params=pltpu.CompilerParams(
            dimension_semantics=("parallel", "parallel", "arbitrary"), vmem_limit_bytes=VMEM_LIMIT),
        name="mlstm_scan",
    )(proj, proj, proj, feat, ftc)


def _post_kernel(*refs, starts):
    x_refs = refs[:len(starts)]
    og_ref, ys_ref, hm_ref, az_ref, sz_ref, co_ref, wo_ref, nw_ref, ln_ref, o_ref = refs[len(starts):]
    oa = og_ref[0].astype(F32) + og_ref[1].astype(F32)
    az = az_ref[...].astype(F32)
    parts = []
    for h in range(GDN_HEADS):
        sl = slice(h * GDN_DV, (h + 1) * GDN_DV)
        seg = oa[:, sl]
        ms = jnp.mean(seg * seg, -1, keepdims=True)
        parts.append(seg * lax.rsqrt(ms + NORM_EPS) * nw_ref[0:1, sl] * _silu(az[:, sl]))
    ys = (ys_ref[0].astype(F32) + ys_ref[1].astype(F32)) * _silu(sz_ref[...].astype(F32))
    gw = HW // SSD_GROUPS
    for g in range(SSD_GROUPS):
        sl = slice(g * gw, (g + 1) * gw)
        seg = ys[:, sl]
        ms = jnp.mean(seg * seg, -1, keepdims=True)
        parts.append(seg * lax.rsqrt(ms + NORM_EPS) * nw_ref[1:2, sl])
    hc = hm_ref[0].astype(F32) + hm_ref[1].astype(F32)
    co = co_ref[...].astype(F32)
    for h in range(MLSTM_HEADS):
        sl = slice(h * MLSTM_DH, (h + 1) * MLSTM_DH)
        seg = hc[:, sl]
        seg = seg - jnp.mean(seg, -1, keepdims=True)
        ms = jnp.mean(seg * seg, -1, keepdims=True)
        parts.append(_sigmoid(co[:, sl]) * (seg * lax.rsqrt(ms + NORM_EPS) * nw_ref[2:3, sl]))
    y = jnp.concatenate(parts, axis=1).astype(BF16)
    hmix = jnp.dot(y, wo_ref[...], preferred_element_type=F32)
    z = DN_ALPHA * _pick_tile(pl.program_id(0), x_refs, starts) + hmix
    o_ref[...] = _layernorm_rows(z, ln_ref[0:1, :], ln_ref[1:2, :])


def _post(og, ys, hm, proj, xs, w_out, norm_w, ln):
    D = w_out.shape[1]
    tb = TB_POST
    x_specs, starts, nt = _row_sources(xs, tb)
    T = nt * tb
    dir_spec = pl.BlockSpec((N_DIR, tb, HW), lambda i: (0, i, 0))
    zc = COL_Z // HW
    return pl.pallas_call(
        functools.partial(_post_kernel, starts=starts),
        grid=(nt,),
        in_specs=x_specs + [dir_spec, dir_spec, dir_spec,
                            pl.BlockSpec((tb, HW), lambda i: (i, zc)),
                            pl.BlockSpec((tb, HW), lambda i: (i, zc + 1)),
                            pl.BlockSpec((tb, HW), lambda i: (i, zc + 2)),
                            pl.BlockSpec(w_out.shape, lambda i: (0, 0)),
                            pl.BlockSpec((SUBLANE, HW), lambda i: (0, 0)),
                            pl.BlockSpec((SUBLANE, D), lambda i: (0, 0))],
        out_specs=pl.BlockSpec((tb, D), lambda i: (i, 0)),
        out_shape=jax.ShapeDtypeStruct((T, D), F32),
        compiler_params=pltpu.CompilerParams(dimension_semantics=("parallel",),
                                             vmem_limit_bytes=VMEM_LIMIT),
        name="post",
    )(*xs, og, ys, hm, proj, proj, proj, w_out, norm_w, ln)


def _swiglu_tile(xb, wg_ref, wu_ref, wd_ref, tf):
    acc = jnp.zeros((xb.shape[0], wd_ref.shape[1]), F32)
    for f0 in range(0, wg_ref.shape[1], tf):
        g = jnp.dot(xb, wg_ref[:, f0:f0 + tf], preferred_element_type=F32)
        u = jnp.dot(xb, wu_ref[:, f0:f0 + tf], preferred_element_type=F32)
        acc = acc + jnp.dot((_silu(g) * u).astype(BF16), wd_ref[f0:f0 + tf, :], preferred_element_type=F32)
    return acc


def _ffn_kernel(x_ref, wg_ref, wu_ref, wd_ref, ln_ref, o_ref, *, tf):
    x = x_ref[...]
    z = DN_ALPHA * x + _swiglu_tile(x.astype(BF16), wg_ref, wu_ref, wd_ref, tf)
    o_ref[...] = _layernorm_rows(z, ln_ref[0:1, :], ln_ref[1:2, :])


def _ffn(x, wg, wu, wd, ln):
    T, D = x.shape
    F = wg.shape[1]
    tm = TM_FFN
    kern = functools.partial(_ffn_kernel, tf=TF_FFN)
    resident = dict(pipeline_mode=pl.Buffered(1))
    return pl.pallas_call(
        kern,
        grid=(T // tm,),
        in_specs=[pl.BlockSpec((tm, D), lambda i: (i, 0)),
                  pl.BlockSpec((D, F), lambda i: (0, 0), **resident),
                  pl.BlockSpec((D, F), lambda i: (0, 0), **resident),
                  pl.BlockSpec((F, D), lambda i: (0, 0), **resident),
                  pl.BlockSpec((SUBLANE, D), lambda i: (0, 0))],
        out_specs=pl.BlockSpec((tm, D), lambda i: (i, 0)),
        out_shape=jax.ShapeDtypeStruct((T, D), F32),
        compiler_params=pltpu.CompilerParams(dimension_semantics=("parallel",),
                                             vmem_limit_bytes=VMEM_LIMIT),
        name="ffn",
    )(x, wg, wu, wd, ln)


I_E1, I_E2, I_R1, I_R2, I_G1, I_G2 = range(6)


def _route_kernel(x_ref, r_ref, info_ref, cnt_ref, run_ref, *, ne):
    @pl.when(pl.program_id(0) == 0)
    def _():
        run_ref[...] = jnp.zeros(run_ref.shape, F32)

    x = x_ref[...]
    w = r_ref[...]
    xh = x.astype(BF16)
    wh = w.astype(BF16)
    xl = (x - xh.astype(F32)).astype(BF16)
    wl = (w - wh.astype(F32)).astype(BF16)
    logits = (jnp.dot(xh, wh, preferred_element_type=F32) + jnp.dot(xl, wh, preferred_element_type=F32)
              + jnp.dot(xh, wl, preferred_element_type=F32))
    tm = logits.shape[0]
    lane = lax.broadcasted_iota(jnp.int32, logits.shape, 1)
    lanef = lane.astype(F32)
    logits = jnp.where(lane < ne, logits, NEG)
    m1 = jnp.max(logits, -1, keepdims=True)
    i1 = jnp.min(jnp.where(logits == m1, lanef, float(LANE)), -1, keepdims=True)
    oh1 = lanef == i1
    rest = jnp.where(oh1, NEG, logits)
    m2 = jnp.max(rest, -1, keepdims=True)
    i2 = jnp.min(jnp.where(rest == m2, lanef, float(LANE)), -1, keepdims=True)
    oh2 = lanef == i2
    e2 = jnp.exp(m2 - m1)
    g1 = 1.0 / (1.0 + e2)
    g2 = e2 / (1.0 + e2)
    occ = jnp.where(oh1, 1.0, 0.0) + jnp.where(oh2, 1.0, 0.0)
    r = lax.broadcasted_iota(jnp.int32, (tm, tm), 0)
    c = lax.broadcasted_iota(jnp.int32, (tm, tm), 1)
    earlier = jnp.where(c < r, 1.0, 0.0).astype(BF16)
    base = run_ref[...] + jnp.dot(earlier, occ.astype(BF16), preferred_element_type=F32)
    rank1 = jnp.sum(jnp.where(oh1, base, 0.0), -1, keepdims=True)
    rank2 = jnp.sum(jnp.where(oh2, base, 0.0), -1, keepdims=True)
    total = run_ref[...] + jnp.sum(occ, axis=0, keepdims=True)
    run_ref[...] = total
    cnt_ref[...] = jnp.broadcast_to(total, cnt_ref.shape)
    info = jnp.zeros(logits.shape, F32)
    for idx, val in ((I_E1, i1), (I_E2, i2), (I_R1, rank1), (I_R2, rank2), (I_G1, g1), (I_G2, g2)):
        info = jnp.where(lane == idx, val, info)
    info_ref[...] = info


def _route(x, router):
    T, D = x.shape
    tm = TM_ROUTE
    kern = functools.partial(_route_kernel, ne=N_EXPERTS)
    return pl.pallas_call(
        kern,
        grid=(T // tm,),
        in_specs=[pl.BlockSpec((tm, D), lambda i: (i, 0)),
                  pl.BlockSpec((D, LANE), lambda i: (0, 0))],
        out_specs=[pl.BlockSpec((tm, LANE), lambda i: (i, 0)),
                   pl.BlockSpec((SUBLANE, LANE), lambda i: (0, 0))],
        out_shape=[jax.ShapeDtypeStruct((T, LANE), F32),
                   jax.ShapeDtypeStruct((SUBLANE, LANE), F32)],
        scratch_shapes=[pltpu.VMEM((1, LANE), F32)],
        compiler_params=pltpu.CompilerParams(dimension_semantics=("arbitrary",),
                                             vmem_limit_bytes=VMEM_LIMIT),
        name="route",
    )(x, router)


def _row_copy(src_ref, src_row, dst_ref, dst_row, sem):
    return pltpu.make_async_copy(src_ref.at[pl.ds(src_row, 1), :], dst_ref.at[pl.ds(dst_row, 1), :], sem)


def _for_rows(tm, fn):
    def body(g, carry):
        for k in range(ROW_UNROLL):
            fn(g * ROW_UNROLL + k, k)
        return carry

    lax.fori_loop(0, tm // ROW_UNROLL, body, 0)


def _dispatch_kernel(p1_ref, p2_ref, x_ref, xs_in_ref, xs_ref, sem, *, tm):
    del xs_in_ref

    def copies(r):
        return (_row_copy(x_ref, r, xs_ref, p1_ref[0, 0, r], sem),
                _row_copy(x_ref, r, xs_ref, p2_ref[0, 0, r], sem))

    def start(r, k):
        for j, cp in enumerate(copies(r)):
            cp.start(priority=(k + j) % 2)

    def wait(r, k):
        for cp in copies(r):
            cp.wait()

    _for_rows(tm, start)
    _for_rows(tm, wait)


def _dispatch(x, pos1, pos2, n_rows):
    T, D = x.shape
    tm = TM_ROUTE
    kern = functools.partial(_dispatch_kernel, tm=tm)
    pos_spec = pl.BlockSpec((1, 1, tm), lambda i: (i, 0, 0), memory_space=pltpu.SMEM)
    return pl.pallas_call(
        kern,
        grid=(T // tm,),
        in_specs=[pos_spec, pos_spec,
                  pl.BlockSpec((tm, D), lambda i: (i, 0)),
                  pl.BlockSpec(memory_space=pl.ANY)],
        out_specs=pl.BlockSpec(memory_space=pl.ANY),
        out_shape=jax.ShapeDtypeStruct((n_rows, D), F32),
        scratch_shapes=[pltpu.SemaphoreType.DMA(())],
        input_output_aliases={3: 0},
        compiler_params=pltpu.CompilerParams(dimension_semantics=("arbitrary",),
                                             vmem_limit_bytes=VMEM_LIMIT),
        name="dispatch",
    )(pos1, pos2, x, jnp.zeros((n_rows, D), F32))


def _gffn_kernel(te_ref, tv_ref, x_ref, wg_ref, wu_ref, wd_ref, y_ref, *, tf):
    del te_ref
    valid = tv_ref[pl.program_id(0)] == 1

    @pl.when(valid)
    def _():
        y_ref[...] = _swiglu_tile(x_ref[...].astype(BF16), wg_ref, wu_ref, wd_ref, tf)

    @pl.when(jnp.logical_not(valid))
    def _():
        y_ref[...] = jnp.zeros(y_ref.shape, F32)


def _gffn(xs, tile_expert, tile_valid, wg, wu, wd):
    P, D = xs.shape
    _, _, F = wg.shape
    tm = TM_GFFN
    kern = functools.partial(_gffn_kernel, tf=TF_MOE)
    grid_spec = pltpu.PrefetchScalarGridSpec(
        num_scalar_prefetch=2,
        grid=(P // tm,),
        in_specs=[pl.BlockSpec((tm, D), lambda i, te, tv: (i, 0)),
                  pl.BlockSpec((None, D, F), lambda i, te, tv: (te[i], 0, 0), pipeline_mode=pl.Buffered(1)),
                  pl.BlockSpec((None, D, F), lambda i, te, tv: (te[i], 0, 0), pipeline_mode=pl.Buffered(1)),
                  pl.BlockSpec((None, F, D), lambda i, te, tv: (te[i], 0, 0), pipeline_mode=pl.Buffered(1))],
        out_specs=pl.BlockSpec((tm, D), lambda i, te, tv: (i, 0)),
    )
    return pl.pallas_call(
        kern,
        grid_spec=grid_spec,
        out_shape=jax.ShapeDtypeStruct((P, D), F32),
        compiler_params=pltpu.CompilerParams(dimension_semantics=("arbitrary",),
                                             vmem_limit_bytes=VMEM_LIMIT),
        name="gffn",
    )(tile_expert, tile_valid, xs, wg, wu, wd)


def _combine_kernel(p1_ref, p2_ref, p1n_ref, p2n_ref, x_ref, info_ref, y_ref, ln_ref, o_ref,
                    ya_ref, yb_ref, sem, *, tm, nt):
    i = pl.program_id(0)
    slot = i % 2

    def copies(pa_ref, pb_ref, r, s):
        return (_row_copy(y_ref, pa_ref[0, 0, r], ya_ref.at[s], r, sem.at[s]),
                _row_copy(y_ref, pb_ref[0, 0, r], yb_ref.at[s], r, sem.at[s]))

    def start_tile(pa_ref, pb_ref, s):
        def start(r, k):
            for j, cp in enumerate(copies(pa_ref, pb_ref, r, s)):
                cp.start(priority=(k + j) % 2)
        _for_rows(tm, start)

    @pl.when(i == 0)
    def _():
        start_tile(p1_ref, p2_ref, 0)

    @pl.when(i + 1 < nt)
    def _():
        start_tile(p1n_ref, p2n_ref, 1 - slot)

    def wait(r, k):
        for cp in copies(p1_ref, p2_ref, r, slot):
            cp.wait()

    _for_rows(tm, wait)
    info = info_ref[...]
    g1 = info[:, I_G1:I_G1 + 1]
    g2 = info[:, I_G2:I_G2 + 1]
    z = DN_ALPHA * x_ref[...] + (g1 * ya_ref[slot] + g2 * yb_ref[slot])
    o_ref[...] = _layernorm_rows(z, ln_ref[0:1, :], ln_ref[1:2, :])


def _combine(x, info, y, pos1, pos2, ln, row0, n_rows):
    _, D = x.shape
    tm = TM_ROUTE
    assert row0 % tm == 0 and n_rows % tm == 0
    t0 = row0 // tm
    nt = n_rows // tm
    kern = functools.partial(_combine_kernel, tm=tm, nt=nt)
    pos_spec = pl.BlockSpec((1, 1, tm), lambda i: (t0 + i, 0, 0), memory_space=pltpu.SMEM)
    next_spec = pl.BlockSpec((1, 1, tm), lambda i: (t0 + jnp.minimum(i + 1, nt - 1), 0, 0),
                             memory_space=pltpu.SMEM)
    return pl.pallas_call(
        kern,
        grid=(nt,),
        in_specs=[pos_spec, pos_spec, next_spec, next_spec,
                  pl.BlockSpec((tm, D), lambda i: (t0 + i, 0)),
                  pl.BlockSpec((tm, LANE), lambda i: (t0 + i, 0)),
                  pl.BlockSpec(memory_space=pl.ANY),
                  pl.BlockSpec((SUBLANE, D), lambda i: (0, 0))],
        out_specs=pl.BlockSpec((tm, D), lambda i: (i, 0)),
        out_shape=jax.ShapeDtypeStruct((n_rows, D), F32),
        scratch_shapes=[pltpu.VMEM((2, tm, D), F32), pltpu.VMEM((2, tm, D), F32),
                        pltpu.SemaphoreType.DMA((2,))],
        compiler_params=pltpu.CompilerParams(dimension_semantics=("arbitrary",),
                                             vmem_limit_bytes=VMEM_LIMIT),
        name="combine",
    )(pos1, pos2, pos1, pos2, x, info, y, ln)


def _moe(x, router, wg, wu, wd, ln, row_ranges):
    T, D = x.shape
    ne = wg.shape[0]
    tm = TM_GFFN
    n_tiles = (N_TOP * T) // tm + ne
    info, cnt = _route(x, router)
    counts = cnt[0, :ne].astype(jnp.int32)
    tiles_e = (counts + tm - 1) // tm
    tile_end = jnp.cumsum(tiles_e)
    offs = (tile_end - tiles_e) * tm
    tid = jnp.arange(n_tiles, dtype=jnp.int32)
    tile_expert = jnp.minimum(jnp.sum(tid[:, None] >= tile_end[None, :], axis=1), ne - 1).astype(jnp.int32)
    tile_valid = (tid < tile_end[-1]).astype(jnp.int32)
    eids = jnp.arange(ne, dtype=jnp.int32)[None, :]

    def slot(e_lane, r_lane):
        e = info[:, e_lane].astype(jnp.int32)
        off = jnp.sum(jnp.where(e[:, None] == eids, offs[None, :], 0), axis=1)
        return (off + info[:, r_lane].astype(jnp.int32)).reshape(T // TM_ROUTE, 1, TM_ROUTE)

    pos1 = slot(I_E1, I_R1)
    pos2 = slot(I_E2, I_R2)
    xs = _dispatch(x, pos1, pos2, n_tiles * tm)
    y = _gffn(xs, tile_expert, tile_valid, wg, wu, wd)
    return [_combine(x, info, y, pos1, pos2, ln, row0, n_rows) for row0, n_rows in row_ranges]


def _pad_rows(a, rows):
    return jnp.concatenate([a, jnp.zeros((rows - a.shape[0],) + a.shape[1:], a.dtype)], axis=0)


def _pack_w_in(w):
    gq = GDN_HEADS * GDN_DK
    gv = GDN_HEADS * GDN_DV
    sw = SSD_HEADS * SSD_HEADDIM
    sbc = SSD_GROUPS * SSD_STATE
    mw = MLSTM_HEADS * MLSTM_DH
    sizes = (gq, gq, gv, gv, N_DIR * GDN_HEADS, N_DIR * GDN_HEADS,
             sw, sw, sbc, sbc, N_DIR * SSD_HEADS,
             mw, mw, mw, mw, N_DIR * MLSTM_HEADS, N_DIR * MLSTM_HEADS)
    idx = np.cumsum(sizes)[:-1].tolist()
    (a_q, a_k, a_v, a_z, a_b, a_a, s_z, s_x, s_b, s_c, s_dt,
     c_q, c_k, c_v, c_o, c_i, c_f) = jnp.split(w, idx, axis=1)
    gates = jnp.concatenate([a_b, a_a, s_dt, s_dt, c_i, c_f], axis=1)
    gates = jnp.concatenate([gates, jnp.zeros((w.shape[0], LANE - gates.shape[1]), w.dtype)], axis=1)
    packed = jnp.concatenate([a_q, a_k, a_v, s_x, s_b, s_c, gates, c_q, c_k, c_v, a_z, s_z, c_o], axis=1)
    return packed.astype(BF16)


def _pack_gate_params(a_A_log, a_dt_bias, b_A_log, b_dt_bias, c_i_bias, c_f_bias):
    z8 = jnp.zeros((N_DIR * GDN_HEADS,), F32)
    z16 = jnp.zeros((N_DIR * SSD_HEADS,), F32)
    bias = jnp.concatenate([z8, a_dt_bias.reshape(-1), b_dt_bias.reshape(-1), b_dt_bias.reshape(-1),
                            c_i_bias.reshape(-1), c_f_bias.reshape(-1)])
    coef = jnp.concatenate([z8, -jnp.exp(a_A_log.reshape(-1)), z16, -jnp.exp(b_A_log.reshape(-1)), z8, z8])
    rows = jnp.stack([bias, coef]).astype(F32)
    rows = jnp.concatenate([rows, jnp.zeros((2, LANE - rows.shape[1]), F32)], axis=1)
    return _pad_rows(rows, SUBLANE)


def _mixer_layer(xs, ns, seq, w_in, conv_a_w, a_A_log, a_dt_bias, a_norm_w, conv_b_w, conv_b_b,
                 b_A_log, b_dt_bias, b_D, b_norm_w, c_i_bias, c_f_bias, c_norm_w, w_out, ln_g, ln_b):
    proj, proj_h = _in_proj(xs, _pack_w_in(w_in))
    conv_w = _pad_rows(jnp.concatenate([conv_a_w, conv_b_w], axis=1).astype(F32), SUBLANE)
    conv_b = jnp.concatenate([jnp.zeros((conv_a_w.shape[1],), F32), conv_b_b.astype(F32)])[None, :]
    gate_p = _pack_gate_params(a_A_log, a_dt_bias, b_A_log, b_dt_bias, c_i_bias, c_f_bias)
    act, feat, ftc = _prep(proj, conv_w, conv_b, gate_p, ns, seq)
    og = _gdn(act, feat, ftc, ns, seq)
    d_row = jnp.repeat(b_D.astype(F32), SSD_HEADDIM)[None, :]
    ys = _ssd(act, feat, ftc, d_row, ns, seq)
    hm = _mlstm(proj_h, feat, ftc, ns, seq)
    norm_w = _pad_rows(jnp.stack([jnp.tile(a_norm_w, GDN_HEADS), b_norm_w, c_norm_w]).astype(F32), SUBLANE)
    ln = _pad_rows(jnp.stack([ln_g, ln_b]).astype(F32), SUBLANE)
    return _post(og, ys, hm, proj_h, xs, w_out.astype(BF16), norm_w, ln)


def kernel(x_prompt, x_sample, w_in, conv_a_w, a_A_log, a_dt_bias, a_norm_w, conv_b_w, conv_b_b, b_A_log, b_dt_bias, b_D, b_norm_w, c_i_bias, c_f_bias, c_norm_w, w_out, ln1_g, ln1_b, ln2_g, ln2_b, ffn_w_gate, ffn_w_up, ffn_w_down, moe_router, moe_w_gate, moe_w_up, moe_w_down):
    bp, seq, dm = x_prompt.shape
    bs, seq_s, _ = x_sample.shape
    assert seq == seq_s and dm == D_MODEL and w_in.shape[0] == DEPTH
    assert seq % TB_MIX == 0 and seq % TB_GDN == 0 and seq % TB_PREP == 0
    ns = bp + bs
    xs = [x_prompt.reshape(bp * seq, dm), x_sample.reshape(bs * seq, dm)]
    for l in range(DEPTH):
        x = _mixer_layer(xs if l == 0 else [x], ns, seq, w_in[l], conv_a_w[l], a_A_log[l], a_dt_bias[l], a_norm_w[l],
                         conv_b_w[l], conv_b_b[l], b_A_log[l], b_dt_bias[l], b_D[l], b_norm_w[l],
                         c_i_bias[l], c_f_bias[l], c_norm_w[l], w_out[l], ln1_g[l], ln1_b[l])
        ln2 = _pad_rows(jnp.stack([ln2_g[l], ln2_b[l]]).astype(F32), SUBLANE)
        j = l // 2
        if l % 2 == 0:
            x = _ffn(x, ffn_w_gate[j].astype(BF16), ffn_w_up[j].astype(BF16), ffn_w_down[j].astype(BF16), ln2)
        else:
            router = jnp.concatenate(
                [moe_router[j], jnp.zeros((dm, LANE - N_EXPERTS), moe_router.dtype)], axis=1).astype(F32)
            ranges = [(0, bp * seq), (bp * seq, bs * seq)] if l == DEPTH - 1 else [(0, ns * seq)]
            outs = _moe(x, router, moe_w_gate[j].astype(BF16), moe_w_up[j].astype(BF16),
                        moe_w_down[j].astype(BF16), ln2, ranges)
            if l == DEPTH - 1:
                return (outs[0].reshape(bp, seq, dm), outs[1].reshape(bs, seq, dm))
            x = outs[0]
    y_prompt = x[:bp * seq].reshape(bp, seq, dm)
    y_sample = x[bp * seq:].reshape(bs, seq, dm)
    return (y_prompt, y_sample)
```
